```python
import jax, jax.numpy as jnp
from jax import lax
import numpy as np

D_MODEL = 2048
BATCH = 2
SEQ = 8192
DEPTH = 2

CHUNK = 64
N_MIXERS = 2
D_FF = 4 * D_MODEL
NORM_EPS = 1e-6

HGRN_EXPAND = 128
HGRN_HEADS = D_MODEL // HGRN_EXPAND
HGRN_DF = HGRN_HEADS * HGRN_EXPAND
HGRN_DV = D_MODEL // HGRN_HEADS

GLA_HEADS = 4
GLA_KEY_DIM = D_MODEL // 2
GLA_VALUE_DIM = D_MODEL
GLA_DK = GLA_KEY_DIM // GLA_HEADS
GLA_DV = GLA_VALUE_DIM // GLA_HEADS
GLA_GATE_RANK = 16
GLA_GATE_NORMALIZER = 16.0

N_HGRN_LAYERS = (DEPTH + 1) // 2
N_GLA_LAYERS = DEPTH // 2

kernel_name = "hybrid_hgrn2_gla_sqrelu_sandwich"


def rmsnorm(x, w):
    xf = x.astype(jnp.float32)
    y = xf * lax.rsqrt(jnp.mean(xf * xf, axis=-1, keepdims=True) + NORM_EPS)
    return (y * w.astype(jnp.float32)).astype(x.dtype)


def gated_head_norm(o, gain, gate):
    o = o * lax.rsqrt(jnp.mean(o * o, axis=-1, keepdims=True) + NORM_EPS)
    o = o * gain.astype(jnp.float32) * jax.nn.silu(gate.astype(jnp.float32))
    B, T, H, dv = o.shape
    return o.reshape(B, T, H * dv)


def chunk_gated_linear_attention(q, k, v, log_decay):
    B, T, H, dk = q.shape
    dv = v.shape[-1]
    n_chunks = T // CHUNK

    def to_chunks(a):
        a = a.astype(jnp.float32).reshape(B, n_chunks, CHUNK, H, a.shape[-1])
        return a.transpose(1, 0, 3, 2, 4)

    causal = jnp.tril(jnp.ones((CHUNK, CHUNK), dtype=bool))[:, :, None]

    def step(S, inp):
        qi, ki, vi, gi = inp
        b = jnp.cumsum(gi, axis=2)
        o_inter = jnp.einsum('bhid,bhde->bhie', qi * jnp.exp(b), S)
        diff = b[:, :, :, None, :] - b[:, :, None, :, :]
        decay = jnp.exp(jnp.where(causal, diff, -jnp.inf))
        scores = jnp.einsum('bhid,bhjd,bhijd->bhij', qi, ki, decay)
        o = o_inter + jnp.einsum('bhij,bhje->bhie', scores, vi)
        b_last = b[:, :, -1:, :]
        S = jnp.exp(b_last[:, :, 0, :])[..., None] * S + jnp.einsum(
            'bhjd,bhje->bhde', ki * jnp.exp(b_last - b), vi)
        return S, o

    S0 = jnp.zeros((B, H, dk, dv), jnp.float32)
    _, o = lax.scan(step, S0, (to_chunks(q), to_chunks(k), to_chunks(v), to_chunks(log_decay)))
    return o.transpose(1, 0, 3, 2, 4).reshape(B, T, H, dv)


def hgrn2_mixer(h, w_in, lower_bound, norm_w, w_out):
    B, T, _ = h.shape
    proj = h @ w_in
    q, f, i, g = jnp.split(proj, [HGRN_DF, 2 * HGRN_DF, 2 * HGRN_DF + D_MODEL], axis=-1)
    q = jax.nn.silu(q.astype(jnp.float32))
    lb = lower_bound.astype(jnp.float32)
    forget = lb + (1.0 - lb) * jax.nn.sigmoid(f.astype(jnp.float32))
    k = 1.0 - forget
    log_f = jnp.log(forget)
    hd = lambda a, d: a.reshape(B, T, HGRN_HEADS, d)
    o = chunk_gated_linear_attention(hd(q, HGRN_EXPAND), hd(k, HGRN_EXPAND),
                                     hd(i, HGRN_DV), hd(log_f, HGRN_EXPAND))
    o = gated_head_norm(o, norm_w, hd(g, HGRN_DV)).astype(h.dtype)
    return o @ w_out


def gla_mixer(h, w_in, w_gk, b_gk, norm_w, w_out):
    B, T, _ = h.shape
    proj = h @ w_in
    q, k, v, g, r = jnp.split(proj, [GLA_KEY_DIM, 2 * GLA_KEY_DIM, 2 * GLA_KEY_DIM + GLA_VALUE_DIM,
                                     2 * GLA_KEY_DIM + 2 * GLA_VALUE_DIM], axis=-1)
    gk = jax.nn.log_sigmoid((r @ w_gk + b_gk).astype(jnp.float32)) / GLA_GATE_NORMALIZER
    q = q.astype(jnp.float32) * (GLA_DK ** -0.5)
    hd = lambda a, d: a.reshape(B, T, GLA_HEADS, d)
    o = chunk_gated_linear_attention(hd(q, GLA_DK), hd(k, GLA_DK), hd(v, GLA_DV), hd(gk, GLA_DK))
    o = gated_head_norm(o, norm_w, hd(g, GLA_DV)).astype(h.dtype)
    return o @ w_out


def sqrelu_mlp(h, w_up, w_down):
    u = jax.nn.relu(h @ w_up)
    return (u * u) @ w_down


def setup_inputs(seed: int = 0) -> dict:
    key = jax.random.key(seed)
    ks = jax.random.split(key, 20)
    nrm = lambda k, shape, s: jax.random.normal(k, shape, jnp.float32) * s
    gain = lambda k, shape: 1.0 + 0.02 * jax.random.normal(k, shape, jnp.float32)
    hgrn_in = 3 * HGRN_DF + D_MODEL
    gla_in = 2 * GLA_KEY_DIM + 2 * GLA_VALUE_DIM + GLA_GATE_RANK
    return {
        "x": jax.random.normal(ks[0], (BATCH, SEQ, D_MODEL), jnp.float32),
        "norm_mix_pre": gain(ks[1], (DEPTH, D_MODEL)),
        "norm_mix_post": gain(ks[2], (DEPTH, D_MODEL)),
        "norm_mlp_pre": gain(ks[3], (DEPTH, D_MODEL)),
        "norm_mlp_post": gain(ks[4], (DEPTH, D_MODEL)),
        "hgrn_w_in": nrm(ks[5], (N_HGRN_LAYERS, D_MODEL, hgrn_in), D_MODEL ** -0.5),
        "hgrn_lb_logits": nrm(ks[6], (DEPTH + 1, HGRN_DF), 0.1),
        "hgrn_norm": gain(ks[7], (N_HGRN_LAYERS, HGRN_DV)),
        "hgrn_w_out": nrm(ks[8], (N_HGRN_LAYERS, D_MODEL, D_MODEL), D_MODEL ** -0.5),
        "gla_w_in": nrm(ks[9], (N_GLA_LAYERS, D_MODEL, gla_in), D_MODEL ** -0.5),
        "gla_w_gk": nrm(ks[10], (N_GLA_LAYERS, GLA_GATE_RANK, GLA_KEY_DIM), GLA_GATE_RANK ** -0.5),
        "gla_b_gk": nrm(ks[11], (N_GLA_LAYERS, GLA_KEY_DIM), 0.1),
        "gla_norm": gain(ks[12], (N_GLA_LAYERS, GLA_DV)),
        "gla_w_out": nrm(ks[13], (N_GLA_LAYERS, D_MODEL, D_MODEL), D_MODEL ** -0.5),
        "mlp_w_up": nrm(ks[14], (DEPTH, D_MODEL, D_FF), D_MODEL ** -0.5),
        "mlp_w_down": nrm(ks[15], (DEPTH, D_FF, D_MODEL), D_FF ** -0.5),
    }


def reference(x, norm_mix_pre, norm_mix_post, norm_mlp_pre, norm_mlp_post,
              hgrn_w_in, hgrn_lb_logits, hgrn_norm, hgrn_w_out,
              gla_w_in, gla_w_gk, gla_b_gk, gla_norm, gla_w_out,
              mlp_w_up, mlp_w_down):
    lower_bounds = jnp.cumsum(jax.nn.softmax(hgrn_lb_logits.astype(jnp.float32), axis=0), axis=0)
    h = x
    for i in range(DEPTH):
        hn = rmsnorm(h, norm_mix_pre[i])
        j = i // N_MIXERS
        if i % N_MIXERS == 0:
            m = hgrn2_mixer(hn, hgrn_w_in[j], lower_bounds[i], hgrn_norm[j], hgrn_w_out[j])
        else:
            m = gla_mixer(hn, gla_w_in[j], gla_w_gk[j], gla_b_gk[j], gla_norm[j], gla_w_out[j])
        h = h + rmsnorm(m, norm_mix_post[i])
        f = sqrelu_mlp(rmsnorm(h, norm_mlp_pre[i]), mlp_w_up[i], mlp_w_down[i])
        h = h + rmsnorm(f, norm_mlp_post[i])
    return h
```

```python
import functools
import math

import numpy as np
import jax
import jax.numpy as jnp
from jax import lax
from jax.experimental import pallas as pl
from jax.experimental.pallas import tpu as pltpu

F32 = jnp.float32
BF16 = jnp.bfloat16

NORM_EPS = 1e-6
CHUNK = 64
HGRN_HEADS = 16
GLA_HEADS = 4
GLA_GATE_NORMALIZER = 16.0

V7X_VMEM_BYTES = 64 * 1024 * 1024
VMEM_HEADROOM_BYTES = 6 * 1024 * 1024
LANES = 128


def _vmem_limit(declared_bytes):
    return int(min(V7X_VMEM_BYTES - 2 * 1024 * 1024, declared_bytes + VMEM_HEADROOM_BYTES))


def _nbytes(shape, dtype):
    return int(np.prod(shape)) * jnp.dtype(dtype).itemsize


def _dot(a, b):
    return jnp.dot(a, b, preferred_element_type=F32)


def _dot_nt(a, b):
    return lax.dot_general(a, b, (((1,), (1,)), ((), ())), preferred_element_type=F32)


def _dot_tn(a, b):
    return lax.dot_general(a, b, (((0,), (0,)), ((), ())), preferred_element_type=F32)


def _sigmoid(x):
    return 1.0 / (1.0 + jnp.exp(-x))


def _silu(x):
    return x * _sigmoid(x)


def _rms_scale(x, w):
    ms = jnp.mean(x * x, axis=-1, keepdims=True)
    return x * lax.rsqrt(ms + NORM_EPS) * w


def _group_out_spec(tm, tn, first_tile, n_tiles):
    def index_map(i, j):
        return i, jnp.clip(j - first_tile, 0, n_tiles - 1)
    return pl.BlockSpec((tm, tn), index_map)


def _hgrn_inproj_kernel(x_ref, nw_ref, w_ref, lbl_ref, q_ref, lf_ref, k_ref, v_ref, gt_ref,
                        xn_ref, *, tiles, layer):
    j = pl.program_id(1)

    @pl.when(j == 0)
    def _():
        xn_ref[...] = _rms_scale(x_ref[...], nw_ref[...]).astype(BF16)

    p = _dot(xn_ref[...], w_ref[...])

    @pl.when(j < tiles)
    def _():
        q_ref[...] = _silu(p)

    @pl.when((j >= tiles) & (j < 2 * tiles))
    def _():
        lg = lbl_ref[...]
        e = jnp.exp(lg - jnp.max(lg, axis=0, keepdims=True))
        lb = jnp.sum(e[:layer + 1], axis=0, keepdims=True) / jnp.sum(e, axis=0, keepdims=True)
        forget = lb + (1.0 - lb) * _sigmoid(p)
        lf_ref[...] = jnp.log(forget)
        k_ref[...] = 1.0 - forget

    @pl.when((j >= 2 * tiles) & (j < 3 * tiles))
    def _():
        v_ref[...] = p.astype(BF16)

    @pl.when(j >= 3 * tiles)
    def _():
        gt_ref[...] = _silu(p)


def _hgrn_inproj(h, norm_w, w_in, lb_logits, layer, *, tm=1024, tn=512):
    m, d = h.shape
    df = w_in.shape[1] // 4
    tiles = df // tn
    kern = functools.partial(_hgrn_inproj_kernel, tiles=tiles, layer=layer)
    f32_out = jax.ShapeDtypeStruct((m, df), F32)
    declared = (2 * _nbytes((tm, d), F32) + _nbytes((tm, d), BF16) + 2 * _nbytes((d, tn), BF16)
                + 2 * (4 * _nbytes((tm, tn), F32) + _nbytes((tm, tn), BF16)))
    return pl.pallas_call(
        kern,
        grid=(m // tm, 4 * tiles),
        in_specs=[
            pl.BlockSpec((tm, d), lambda i, j: (i, 0)),
            pl.BlockSpec((1, d), lambda i, j: (0, 0)),
            pl.BlockSpec((d, tn), lambda i, j: (0, j)),
            pl.BlockSpec((lb_logits.shape[0], tn),
                         lambda i, j: (0, jnp.clip(j - tiles, 0, tiles - 1))),
        ],
        out_specs=[
            _group_out_spec(tm, tn, 0, tiles),
            _group_out_spec(tm, tn, tiles, tiles),
            _group_out_spec(tm, tn, tiles, tiles),
            _group_out_spec(tm, tn, 2 * tiles, tiles),
            _group_out_spec(tm, tn, 3 * tiles, tiles),
        ],
        out_shape=[f32_out, f32_out, f32_out, jax.ShapeDtypeStruct((m, df), BF16), f32_out],
        scratch_shapes=[pltpu.VMEM((tm, d), BF16)],
        compiler_params=pltpu.CompilerParams(
            dimension_semantics=("arbitrary", "arbitrary"),
            vmem_limit_bytes=_vmem_limit(declared)),
        name="hgrn_inproj",
    )(h, norm_w.reshape(1, d), w_in, lb_logits)


def _split_bf16(x):
    hi = x.astype(BF16)
    return hi, (x - hi.astype(F32)).astype(BF16)


def _gla_inproj_kernel(x_ref, nw_ref, w_ref, wr_ref, wgk_ref, bgk_ref,
                       q_ref, k_ref, v_ref, gt_ref, gk_ref, xn_ref, r_ref, *, kt, vt, q_scale):
    j = pl.program_id(1)

    @pl.when(j == 0)
    def _():
        xn = _rms_scale(x_ref[...], nw_ref[...]).astype(BF16)
        xn_ref[...] = xn
        r_ref[...] = _dot(xn, wr_ref[...])

    p = _dot(xn_ref[...], w_ref[...])

    @pl.when(j < kt)
    def _():
        q_ref[...] = p * q_scale
        r_hi, r_lo = _split_bf16(r_ref[...])
        w_hi, w_lo = _split_bf16(wgk_ref[...])
        z = _dot(r_hi, w_hi) + (_dot(r_hi, w_lo) + _dot(r_lo, w_hi)) + bgk_ref[...]
        log_sig = jnp.minimum(z, 0.0) - jnp.log(1.0 + jnp.exp(-jnp.abs(z)))
        gk_ref[...] = log_sig / GLA_GATE_NORMALIZER

    @pl.when((j >= kt) & (j < 2 * kt))
    def _():
        k_ref[...] = p

    @pl.when((j >= 2 * kt) & (j < 2 * kt + vt))
    def _():
        v_ref[...] = p.astype(BF16)

    @pl.when(j >= 2 * kt + vt)
    def _():
        gt_ref[...] = _silu(p)


def _gla_inproj(h, norm_w, w_main, w_r, w_gk, b_gk, *, key_dim, value_dim, heads, tm=1024, tn=512):
    m, d = h.shape
    kt, vt = key_dim // tn, value_dim // tn
    kern = functools.partial(_gla_inproj_kernel, kt=kt, vt=vt,
                             q_scale=float((key_dim // heads) ** -0.5))
    declared = (2 * _nbytes((tm, d), F32) + _nbytes((tm, d), BF16) + 2 * _nbytes((d, tn), BF16)
                + 2 * _nbytes((d, LANES), BF16) + _nbytes((tm, LANES), F32)
                + 2 * (4 * _nbytes((tm, tn), F32) + _nbytes((tm, tn), BF16)))
    return pl.pallas_call(
        kern,
        grid=(m // tm, 2 * kt + 2 * vt),
        in_specs=[
            pl.BlockSpec((tm, d), lambda i, j: (i, 0)),
            pl.BlockSpec((1, d), lambda i, j: (0, 0)),
            pl.BlockSpec((d, tn), lambda i, j: (0, j)),
            pl.BlockSpec((d, LANES), lambda i, j: (0, 0)),
            pl.BlockSpec((LANES, tn), lambda i, j: (0, jnp.minimum(j, kt - 1))),
            pl.BlockSpec((1, tn), lambda i, j: (0, jnp.minimum(j, kt - 1))),
        ],
        out_specs=[
            _group_out_spec(tm, tn, 0, kt),
            _group_out_spec(tm, tn, kt, kt),
            _group_out_spec(tm, tn, 2 * kt, vt),
            _group_out_spec(tm, tn, 2 * kt + vt, vt),
            _group_out_spec(tm, tn, 0, kt),
        ],
        out_shape=[
            jax.ShapeDtypeStruct((m, key_dim), F32),
            jax.ShapeDtypeStruct((m, key_dim), F32),
            jax.ShapeDtypeStruct((m, value_dim), BF16),
            jax.ShapeDtypeStruct((m, value_dim), F32),
            jax.ShapeDtypeStruct((m, key_dim), F32),
        ],
        scratch_shapes=[pltpu.VMEM((tm, d), BF16), pltpu.VMEM((tm, LANES), F32)],
        compiler_params=pltpu.CompilerParams(
            dimension_semantics=("arbitrary", "arbitrary"),
            vmem_limit_bytes=_vmem_limit(declared)),
        name="gla_inproj",
    )(h, norm_w.reshape(1, d), w_main, w_r, w_gk, b_gk.reshape(1, key_dim))


def _pair_levels(c):
    i = np.arange(c)[:, None]
    j = np.arange(c)[None, :]
    x = i ^ j
    lvl = np.where(x > 0, np.floor(np.log2(np.maximum(x, 1))).astype(np.int64) + 1, 0)
    return np.where(j > i, -1, lvl).astype(np.int32)


def _reference_rows(b, b_ref, row, s, c):
    dk = b.shape[-1]
    if s == 1:
        return jnp.where((row & 1) != 0, pltpu.roll(b, 1, 0), b)
    if s == 2:
        lo = [jnp.broadcast_to(b_ref[pl.ds(8 * v + 1, 1), :], (8, dk)) for v in range(c // 8)]
        hi = [jnp.broadcast_to(b_ref[pl.ds(8 * v + 5, 1), :], (8, dk)) for v in range(c // 8)]
        return jnp.where((row & 4) != 0, jnp.concatenate(hi, axis=0), jnp.concatenate(lo, axis=0))
    parts = [jnp.broadcast_to(b_ref[pl.ds(m * 2 * s + s - 1, 1), :], (2 * s, dk))
             for m in range(c // (2 * s))]
    return parts[0] if len(parts) == 1 else jnp.concatenate(parts, axis=0)


def _gla_chunk_kernel(lvl_ref, q_ref, k_ref, g_ref, v_ref, gt_ref, gain_ref, o_ref,
                      st_ref, b_ref, *, chunk, n_chunks):
    c = chunk
    dk = q_ref.shape[-1]

    @pl.when(pl.program_id(2) == 0)
    def _():
        st_ref[...] = jnp.zeros_like(st_ref)

    lvl = lvl_ref[...]
    tril = jnp.where(lvl >= 0, 1.0, 0.0).astype(BF16)
    row = lax.broadcasted_iota(jnp.int32, (c, dk), 0)
    gain = gain_ref[...]

    def body(ci, carry):
        rows = pl.ds(pl.multiple_of(ci * c, c), c)
        q = q_ref[rows, :]
        k = k_ref[rows, :]
        g = g_ref[rows, :]
        v = v_ref[rows, :]

        g1 = g.astype(BF16)
        r1 = g - g1.astype(F32)
        g2 = r1.astype(BF16)
        g3 = (r1 - g2.astype(F32)).astype(BF16)
        b = _dot(tril, g1) + (_dot(tril, g2) + _dot(tril, g3))
        b_ref[...] = b
        b_last = b_ref[pl.ds(c - 1, 1), :]

        st = st_ref[...]
        o = _dot_nt((q * jnp.exp(b)).astype(BF16), st.astype(BF16))

        scores = jnp.where(lvl == 0, _dot_nt(q.astype(BF16), k.astype(BF16)), 0.0)
        s = 1
        level = 1
        while s < c:
            e = jnp.exp(-jnp.abs(b - _reference_rows(b, b_ref, row, s, c)))
            upper = (row & s) != 0
            qe = jnp.where(upper, q * e, 0.0).astype(BF16)
            ke = jnp.where(upper, 0.0, k * e).astype(BF16)
            scores = jnp.where(lvl == level, _dot_nt(qe, ke), scores)
            s *= 2
            level += 1
        o = o + _dot(scores.astype(BF16), v)

        kl = (k * jnp.exp(b_last - b)).astype(BF16)
        st_ref[...] = st * jnp.exp(b_last) + _dot_tn(v, kl)

        on = _rms_scale(o, gain) * gt_ref[rows, :]
        o_ref[rows, :] = on.astype(o_ref.dtype)
        return carry

    lax.fori_loop(0, n_chunks, body, 0)


def _gla_chunk(q, k, g, v, gate, gain, *, batch, heads, rows_per_step=2048, chunk=CHUNK):
    m, kd = q.shape
    vd = v.shape[1]
    dk, dv = kd // heads, vd // heads
    seq = m // batch
    tc = min(rows_per_step, seq)
    steps = seq // tc
    kern = functools.partial(_gla_chunk_kernel, chunk=chunk, n_chunks=tc // chunk)
    lvl = jnp.asarray(_pair_levels(chunk))

    def rows_map(b, h, t):
        return b * steps + t, h

    declared = (2 * (3 * _nbytes((tc, dk), F32) + _nbytes((tc, dv), BF16) + _nbytes((tc, dv), F32)
                     + _nbytes((tc, dv), BF16))
                + _nbytes((dv, dk), F32) + _nbytes((chunk, dk), F32))
    return pl.pallas_call(
        kern,
        grid=(batch, heads, steps),
        in_specs=[
            pl.BlockSpec((chunk, chunk), lambda b, h, t: (0, 0)),
            pl.BlockSpec((tc, dk), rows_map),
            pl.BlockSpec((tc, dk), rows_map),
            pl.BlockSpec((tc, dk), rows_map),
            pl.BlockSpec((tc, dv), rows_map),
            pl.BlockSpec((tc, dv), rows_map),
            pl.BlockSpec((1, dv), lambda b, h, t: (0, 0)),
        ],
        out_specs=pl.BlockSpec((tc, dv), rows_map),
        out_shape=jax.ShapeDtypeStruct((m, vd), BF16),
        scratch_shapes=[pltpu.VMEM((dv, dk), F32), pltpu.VMEM((chunk, dk), F32)],
        compiler_params=pltpu.CompilerParams(
            dimension_semantics=("arbitrary", "arbitrary", "arbitrary"),
            vmem_limit_bytes=_vmem_limit(declared)),
        name="gla_chunk",
    )(lvl, q, k, g, v, gate, gain.reshape(1, dv))


def _outproj_kernel(o_ref, w_ref, h_ref, nw_ref, out_ref):
    mixed = _dot(o_ref[...], w_ref[...])
    out_ref[...] = h_ref[...] + _rms_scale(mixed, nw_ref[...])


def _outproj(o, w_out, h, norm_w, *, tm=512):
    m, d = h.shape
    dv = o.shape[1]
    declared = (2 * _nbytes((tm, dv), BF16) + 2 * _nbytes((dv, d), BF16) + 4 * _nbytes((tm, d), F32))
    return pl.pallas_call(
        _outproj_kernel,
        grid=(m // tm,),
        in_specs=[
            pl.BlockSpec((tm, dv), lambda i: (i, 0)),
            pl.BlockSpec((dv, d), lambda i: (0, 0)),
            pl.BlockSpec((tm, d), lambda i: (i, 0)),
            pl.BlockSpec((1, d), lambda i: (0, 0)),
        ],
        out_specs=pl.BlockSpec((tm, d), lambda i: (i, 0)),
        out_shape=jax.ShapeDtypeStruct((m, d), F32),
        compiler_params=pltpu.CompilerParams(
            dimension_semantics=("arbitrary",),
            vmem_limit_bytes=_vmem_limit(declared)),
        name="outproj",
    )(o, w_out, h, norm_w.reshape(1, d))


def _mlp_kernel(h_ref, pre_ref, wu_ref, wd_ref, post_ref, out_ref, xn_ref, acc_ref):
    kstep = pl.program_id(1)

    @pl.when(kstep == 0)
    def _():
        xn_ref[...] = _rms_scale(h_ref[...], pre_ref[...]).astype(BF16)
        acc_ref[...] = jnp.zeros_like(acc_ref)

    u = jnp.maximum(_dot(xn_ref[...], wu_ref[...]), 0.0)
    acc_ref[...] += _dot((u * u).astype(BF16), wd_ref[...])

    @pl.when(kstep == pl.num_programs(1) - 1)
    def _():
        out_ref[...] = h_ref[...] + _rms_scale(acc_ref[...], post_ref[...])


def _mlp(h, pre_w, w_up, w_down, post_w, *, tm=512, tk=1024):
    m, d = h.shape
    dff = w_up.shape[1]
    declared = (4 * _nbytes((tm, d), F32) + 2 * _nbytes((d, tk), BF16) + 2 * _nbytes((tk, d), BF16)
                + _nbytes((tm, d), BF16) + _nbytes((tm, d), F32) + 2 * _nbytes((tm, tk), F32))
    return pl.pallas_call(
        _mlp_kernel,
        grid=(m // tm, dff // tk),
        in_specs=[
            pl.BlockSpec((tm, d), lambda i, k: (i, 0)),
            pl.BlockSpec((1, d), lambda i, k: (0, 0)),
            pl.BlockSpec((d, tk), lambda i, k: (0, k)),
            pl.BlockSpec((tk, d), lambda i, k: (k, 0)),
            pl.BlockSpec((1, d), lambda i, k: (0, 0)),
        ],
        out_specs=pl.BlockSpec((tm, d), lambda i, k: (i, 0)),
        out_shape=jax.ShapeDtypeStruct((m, d), F32),
        scratch_shapes=[pltpu.VMEM((tm, d), BF16), pltpu.VMEM((tm, d), F32)],
        compiler_params=pltpu.CompilerParams(
            dimension_semantics=("arbitrary", "arbitrary"),
            vmem_limit_bytes=_vmem_limit(declared)),
        name="mlp",
    )(h, pre_w.reshape(1, d), w_up, w_down, post_w.reshape(1, d))


def kernel(x, norm_mix_pre, norm_mix_post, norm_mlp_pre, norm_mlp_post, hgrn_w_in, hgrn_lb_logits, hgrn_norm, hgrn_w_out, gla_w_in, gla_w_gk, gla_b_gk, gla_norm, gla_w_out, mlp_w_up, mlp_w_down):
    batch, seq, d = x.shape
    depth = norm_mix_pre.shape[0]
    h = x.reshape(batch * seq, d)
    for layer in range(depth):
        j = layer // 2
        if layer % 2 == 0:
            q, lf, k, v, gate = _hgrn_inproj(h, norm_mix_pre[layer], hgrn_w_in[j].astype(BF16),
                                             hgrn_lb_logits, layer)
            o = _gla_chunk(q, k, lf, v, gate, hgrn_norm[j], batch=batch, heads=HGRN_HEADS)
            w_out = hgrn_w_out[j]
        else:
            key_dim = gla_w_gk.shape[2]
            rank = gla_w_gk.shape[1]
            value_dim = (gla_w_in.shape[2] - 2 * key_dim - rank) // 2
            n_main = 2 * key_dim + 2 * value_dim
            w_main = gla_w_in[j, :, :n_main].astype(BF16)
            w_r = jnp.pad(gla_w_in[j, :, n_main:], ((0, 0), (0, LANES - rank))).astype(BF16)
            w_gk = jnp.pad(gla_w_gk[j], ((0, LANES - rank), (0, 0)))
            q, k, v, gate, gk = _gla_inproj(h, norm_mix_pre[layer], w_main, w_r, w_gk, gla_b_gk[j],
                                            key_dim=key_dim, value_dim=value_dim, heads=GLA_HEADS)
            o = _gla_chunk(q, k, gk, v, gate, gla_norm[j], batch=batch, heads=GLA_HEADS)
            w_out = gla_w_out[j]
        h = _outproj(o, w_out.astype(BF16), h, norm_mix_post[layer])
        h = _mlp(h, norm_mlp_pre[layer], mlp_w_up[layer].astype(BF16), mlp_w_down[layer].astype(BF16),
                 norm_mlp_post[layer])
    return h.reshape(batch, seq, d)
```

```python
import functools
import math

import numpy as np
import jax
import jax.numpy as jnp
from jax import lax
from jax.experimental import pallas as pl
from jax.experimental.pallas import tpu as pltpu

F32 = jnp.float32
BF16 = jnp.bfloat16

NORM_EPS = 1e-6
CHUNK = 64
HGRN_HEADS = 16
GLA_HEADS = 4
GLA_GATE_NORMALIZER = 16.0
LOG2_E = math.log2(math.e)

V7X_VMEM_BYTES = 64 * 1024 * 1024
VMEM_HEADROOM_BYTES = 6 * 1024 * 1024
LANES = 128
MLP_DOWN_COLS = 512


def _vmem_limit(declared_bytes):
    return int(min(V7X_VMEM_BYTES - 2 * 1024 * 1024, declared_bytes + VMEM_HEADROOM_BYTES))


def _nbytes(shape, dtype):
    return int(np.prod(shape)) * jnp.dtype(dtype).itemsize


def _dot(a, b):
    return jnp.dot(a, b, preferred_element_type=F32)


def _dot_nt(a, b):
    return lax.dot_general(a, b, (((1,), (1,)), ((), ())), preferred_element_type=F32)


def _dot_tn(a, b):
    return lax.dot_general(a, b, (((0,), (0,)), ((), ())), preferred_element_type=F32)


def _sigmoid(x):
    return 1.0 / (1.0 + jnp.exp(-x))


def _silu(x):
    return x * _sigmoid(x)


def _rms_scale(x, w):
    ms = jnp.mean(x * x, axis=-1, keepdims=True)
    return x * lax.rsqrt(ms + NORM_EPS) * w


def _split3_bf16(x):
    a = x.astype(BF16)
    r = x - a.astype(F32)
    b = r.astype(BF16)
    return a, b, (r - b.astype(F32)).astype(BF16)


def _hgrn_inproj_kernel(x_ref, nw_ref, wq_ref, wf_ref, wi_ref, wg_ref, lbl_ref, gain_ref,
                        q_ref, k_ref, g1_ref, g2_ref, g3_ref, v_ref, gt_ref, xn_ref, *, layer):
    @pl.when(pl.program_id(1) == 0)
    def _():
        xn_ref[...] = _rms_scale(x_ref[...], nw_ref[...]).astype(BF16)

    xn = xn_ref[...]
    q_ref[...] = _silu(_dot(xn, wq_ref[...])).astype(BF16)
    lg = lbl_ref[...]
    e = jnp.exp(lg - jnp.max(lg, axis=0, keepdims=True))
    lb = jnp.sum(e[:layer + 1], axis=0, keepdims=True) / jnp.sum(e, axis=0, keepdims=True)
    forget = lb + (1.0 - lb) * _sigmoid(_dot(xn, wf_ref[...]))
    k_ref[...] = (1.0 - forget).astype(BF16)
    g1_ref[...], g2_ref[...], g3_ref[...] = _split3_bf16(jnp.log2(forget))
    v_ref[...] = _dot(xn, wi_ref[...]).astype(BF16)
    gt_ref[...] = _silu(_dot(xn, wg_ref[...])) * gain_ref[...]


def _hgrn_inproj(h, norm_w, w_in, lb_logits, gain_row, layer, *, tm=1024, tn=256):
    m, d = h.shape
    df = w_in.shape[1] // 4
    tiles = df // tn
    kern = functools.partial(_hgrn_inproj_kernel, layer=layer)
    bf_out = jax.ShapeDtypeStruct((m, df), BF16)
    declared = (2 * _nbytes((tm, d), F32) + _nbytes((tm, d), BF16) + 8 * _nbytes((d, tn), BF16)
                + 2 * (6 * _nbytes((tm, tn), BF16) + _nbytes((tm, tn), F32)))
    w_spec = [pl.BlockSpec((d, tn), functools.partial(lambda i, j, g: (0, g * tiles + j), g=g))
              for g in range(4)]
    out_spec = pl.BlockSpec((tm, tn), lambda i, j: (i, j))
    return pl.pallas_call(
        kern,
        grid=(m // tm, tiles),
        in_specs=[
            pl.BlockSpec((tm, d), lambda i, j: (i, 0)),
            pl.BlockSpec((1, d), lambda i, j: (0, 0)),
            *w_spec,
            pl.BlockSpec((lb_logits.shape[0], tn), lambda i, j: (0, j)),
            pl.BlockSpec((1, tn), lambda i, j: (0, j)),
        ],
        out_specs=[out_spec] * 7,
        out_shape=[bf_out] * 6 + [jax.ShapeDtypeStruct((m, df), F32)],
        scratch_shapes=[pltpu.VMEM((tm, d), BF16)],
        compiler_params=pltpu.CompilerParams(
            dimension_semantics=("arbitrary", "arbitrary"),
            vmem_limit_bytes=_vmem_limit(declared)),
        name="hgrn_inproj",
    )(h, norm_w.reshape(1, d), w_in, w_in, w_in, w_in, lb_logits, gain_row)


def _gla_inproj_kernel(x_ref, nw_ref, wq_ref, wk_ref, wv_ref, wg_ref, wr_ref, wgk_ref, bgk_ref,
                       gain_ref, q_ref, k_ref, g1_ref, g2_ref, g3_ref, v_ref, gt_ref,
                       xn_ref, r_ref, *, q_scale):
    @pl.when(pl.program_id(1) == 0)
    def _():
        xn0 = _rms_scale(x_ref[...], nw_ref[...]).astype(BF16)
        xn_ref[...] = xn0
        r_ref[...] = _dot(xn0, wr_ref[...])

    xn = xn_ref[...]
    q_ref[...] = (_dot(xn, wq_ref[...]) * q_scale).astype(BF16)
    k_ref[...] = _dot(xn, wk_ref[...]).astype(BF16)
    v_ref[...] = _dot(xn, wv_ref[...]).astype(BF16)
    gt_ref[...] = _silu(_dot(xn, wg_ref[...])) * gain_ref[...]
    r = r_ref[...]
    w = wgk_ref[...]
    r_hi = r.astype(BF16)
    r_lo = (r - r_hi.astype(F32)).astype(BF16)
    w_hi = w.astype(BF16)
    w_lo = (w - w_hi.astype(F32)).astype(BF16)
    z = _dot(r_hi, w_hi) + (_dot(r_hi, w_lo) + _dot(r_lo, w_hi)) + bgk_ref[...]
    log_sig = jnp.minimum(z, 0.0) - jnp.log(1.0 + jnp.exp(-jnp.abs(z)))
    g1_ref[...], g2_ref[...], g3_ref[...] = _split3_bf16(log_sig * (LOG2_E / GLA_GATE_NORMALIZER))


def _gla_inproj(h, norm_w, w_in, w_r, w_gk, b_gk, gain_row, *, key_dim, value_dim, heads,
                tm=1024, steps=4):
    m, d = h.shape
    tk, tv = key_dim // steps, value_dim // steps
    kern = functools.partial(_gla_inproj_kernel, q_scale=float((key_dim // heads) ** -0.5))
    declared = (2 * _nbytes((tm, d), F32) + _nbytes((tm, d), BF16)
                + 4 * _nbytes((d, tk), BF16) + 4 * _nbytes((d, tv), BF16)
                + 2 * _nbytes((d, LANES), BF16) + _nbytes((tm, LANES), F32)
                + 2 * (5 * _nbytes((tm, tk), BF16) + _nbytes((tm, tv), F32) + _nbytes((tm, tv), BF16)))
    k_first, v_first, g_first = key_dim // tk, 2 * key_dim // tv, (2 * key_dim + value_dim) // tv
    key_out = pl.BlockSpec((tm, tk), lambda i, j: (i, j))
    val_out = pl.BlockSpec((tm, tv), lambda i, j: (i, j))
    key_shape = jax.ShapeDtypeStruct((m, key_dim), BF16)
    return pl.pallas_call(
        kern,
        grid=(m // tm, steps),
        in_specs=[
            pl.BlockSpec((tm, d), lambda i, j: (i, 0)),
            pl.BlockSpec((1, d), lambda i, j: (0, 0)),
            pl.BlockSpec((d, tk), lambda i, j: (0, j)),
            pl.BlockSpec((d, tk), lambda i, j: (0, k_first + j)),
            pl.BlockSpec((d, tv), lambda i, j: (0, v_first + j)),
            pl.BlockSpec((d, tv), lambda i, j: (0, g_first + j)),
            pl.BlockSpec((d, LANES), lambda i, j: (0, 0)),
            pl.BlockSpec((LANES, tk), lambda i, j: (0, j)),
            pl.BlockSpec((1, tk), lambda i, j: (0, j)),
            pl.BlockSpec((1, tv), lambda i, j: (0, j)),
        ],
        out_specs=[key_out] * 5 + [val_out, val_out],
        out_shape=[key_shape] * 5 + [jax.ShapeDtypeStruct((m, value_dim), BF16),
                                     jax.ShapeDtypeStruct((m, value_dim), F32)],
        scratch_shapes=[pltpu.VMEM((tm, d), BF16), pltpu.VMEM((tm, LANES), F32)],
        compiler_params=pltpu.CompilerParams(
            dimension_semantics=("arbitrary", "arbitrary"),
            vmem_limit_bytes=_vmem_limit(declared)),
        name="gla_inproj",
    )(h, norm_w.reshape(1, d), w_in, w_in, w_in, w_in, w_r, w_gk, b_gk.reshape(1, key_dim), gain_row)


def _pair_levels(c):
    i = np.arange(c)[:, None]
    j = np.arange(c)[None, :]
    x = i ^ j
    lvl = np.where(x > 0, np.floor(np.log2(np.maximum(x, 1))).astype(np.int64) + 1, 0)
    return np.where(j > i, -1, lvl).astype(np.int32)


def _reference_rows(b, b_ref, row, s, c):
    dk = b.shape[-1]
    if s == 1:
        return jnp.where((row & 1) != 0, pltpu.roll(b, 1, 0), b)
    if s == 2:
        lo = [jnp.broadcast_to(b_ref[pl.ds(8 * v + 1, 1), :], (8, dk)) for v in range(c // 8)]
        hi = [jnp.broadcast_to(b_ref[pl.ds(8 * v + 5, 1), :], (8, dk)) for v in range(c // 8)]
        return jnp.where((row & 4) != 0, jnp.concatenate(hi, axis=0), jnp.concatenate(lo, axis=0))
    parts = [jnp.broadcast_to(b_ref[pl.ds(m * 2 * s + s - 1, 1), :], (2 * s, dk))
             for m in range(c // (2 * s))]
    return parts[0] if len(parts) == 1 else jnp.concatenate(parts, axis=0)


def _gla_chunk_kernel(lvl_ref, q_ref, k_ref, g1_ref, g2_ref, g3_ref, v_ref, gt_ref, o_ref,
                      st_ref, b_ref, *, chunk, n_chunks, group):
    c = chunk
    dk = q_ref.shape[-1] // group
    dv = v_ref.shape[-1] // group

    @pl.when(pl.program_id(2) == 0)
    def _():
        st_ref[...] = jnp.zeros_like(st_ref)

    lvl = lvl_ref[...]
    tril = jnp.where(lvl >= 0, 1.0, 0.0).astype(BF16)
    row = lax.broadcasted_iota(jnp.int32, (c, dk), 0)
    heads = range(group)
    kcols = [slice(hh * dk, (hh + 1) * dk) for hh in heads]
    vcols = [slice(hh * dv, (hh + 1) * dv) for hh in heads]

    def body(ci, carry):
        rows = pl.ds(pl.multiple_of(ci * c, c), c)
        q = [q_ref[rows, kcols[hh]] for hh in heads]
        k = [k_ref[rows, kcols[hh]] for hh in heads]
        v = [v_ref[rows, vcols[hh]] for hh in heads]

        b = [_dot(tril, g1_ref[rows, kcols[hh]])
             + (_dot(tril, g2_ref[rows, kcols[hh]]) + _dot(tril, g3_ref[rows, kcols[hh]]))
             for hh in heads]
        for hh in heads:
            b_ref[hh] = b[hh]
        b_last = [b_ref[hh, pl.ds(c - 1, 1), :] for hh in heads]

        st = [st_ref[hh] for hh in heads]
        o = [_dot_nt(q[hh] * jnp.exp2(b[hh]).astype(BF16), st[hh].astype(BF16)) for hh in heads]

        scores = [jnp.where(lvl == 0, _dot_nt(q[hh], k[hh]), 0.0) for hh in heads]
        s = 1
        level = 1
        while s < c:
            for hh in heads:
                dist = jnp.abs(b[hh] - _reference_rows(b[hh], b_ref.at[hh], row, s, c))
                e = jnp.exp2(-dist).astype(BF16)
                scores[hh] = jnp.where(lvl == level, _dot_nt(q[hh] * e, k[hh] * e), scores[hh])
            s *= 2
            level += 1
        o = [o[hh] + _dot(scores[hh].astype(BF16), v[hh]) for hh in heads]

        for hh in heads:
            kl = k[hh] * jnp.exp2(b_last[hh] - b[hh]).astype(BF16)
            st_ref[hh] = st[hh] * jnp.exp2(b_last[hh]) + _dot_tn(v[hh], kl)

        for hh in heads:
            ms = jnp.mean(o[hh] * o[hh], axis=-1, keepdims=True)
            on = o[hh] * lax.rsqrt(ms + NORM_EPS) * gt_ref[rows, vcols[hh]]
            o_ref[rows, vcols[hh]] = on.astype(o_ref.dtype)
        return carry

    lax.fori_loop(0, n_chunks, body, 0)


def _gla_chunk(q, k, g_split, v, gate, *, batch, heads, group, rows_per_step, chunk=CHUNK):
    m, kd = q.shape
    vd = v.shape[1]
    dk, dv = kd // heads, vd // heads
    gk, gv = group * dk, group * dv
    seq = m // batch
    tc = min(rows_per_step, seq)
    steps = seq // tc
    kern = functools.partial(_gla_chunk_kernel, chunk=chunk, n_chunks=tc // chunk, group=group)
    lvl = jnp.asarray(_pair_levels(chunk))

    def rows_map(b, h, t):
        return b * steps + t, h

    declared = (2 * (5 * _nbytes((tc, gk), BF16) + 2 * _nbytes((tc, gv), BF16) + _nbytes((tc, gv), F32))
                + _nbytes((group, dv, dk), F32) + _nbytes((group, chunk, dk), F32))
    key_spec = pl.BlockSpec((tc, gk), rows_map)
    val_spec = pl.BlockSpec((tc, gv), rows_map)
    return pl.pallas_call(
        kern,
        grid=(batch, heads // group, steps),
        in_specs=[pl.BlockSpec((chunk, chunk), lambda b, h, t: (0, 0))] + [key_spec] * 5 + [val_spec] * 2,
        out_specs=val_spec,
        out_shape=jax.ShapeDtypeStruct((m, vd), BF16),
        scratch_shapes=[pltpu.VMEM((group, dv, dk), F32), pltpu.VMEM((group, chunk, dk), F32)],
        compiler_params=pltpu.CompilerParams(
            dimension_semantics=("arbitrary", "arbitrary", "arbitrary"),
            vmem_limit_bytes=_vmem_limit(declared)),
        name="gla_chunk",
    )(lvl, q, k, *g_split, v, gate)


def _outproj_kernel(o_ref, w_ref, h_ref, nw_ref, out_ref):
    mixed = _dot(o_ref[...], w_ref[...])
    out_ref[...] = h_ref[...] + _rms_scale(mixed, nw_ref[...])


def _outproj(o, w_out, h, norm_w, *, tm=512):
    m, d = h.shape
    dv = o.shape[1]
    declared = (2 * _nbytes((tm, dv), BF16) + 2 * _nbytes((dv, d), BF16) + 4 * _nbytes((tm, d), F32))
    return pl.pallas_call(
        _outproj_kernel,
        grid=(m // tm,),
        in_specs=[
            pl.BlockSpec((tm, dv), lambda i: (i, 0)),
            pl.BlockSpec((dv, d), lambda i: (0, 0)),
            pl.BlockSpec((tm, d), lambda i: (i, 0)),
            pl.BlockSpec((1, d), lambda i: (0, 0)),
        ],
        out_specs=pl.BlockSpec((tm, d), lambda i: (i, 0)),
        out_shape=jax.ShapeDtypeStruct((m, d), F32),
        compiler_params=pltpu.CompilerParams(
            dimension_semantics=("arbitrary",),
            vmem_limit_bytes=_vmem_limit(declared)),
        name="outproj",
    )(o, w_out, h, norm_w.reshape(1, d))


def _mlp_kernel(h_ref, pre_ref, wu_ref, wd_ref, post_ref, out_ref, xn_ref):
    kstep = pl.program_id(1)

    @pl.when(kstep == 0)
    def _():
        xn_ref[...] = _rms_scale(h_ref[...], pre_ref[...]).astype(BF16)
        out_ref[...] = jnp.zeros_like(out_ref)

    u = jnp.maximum(_dot(xn_ref[...], wu_ref[...]), 0.0)
    u2 = (u * u).astype(BF16)
    for n in range(0, out_ref.shape[1], MLP_DOWN_COLS):
        cols = slice(n, n + MLP_DOWN_COLS)
        out_ref[:, cols] += _dot(u2, wd_ref[:, cols])

    @pl.when(kstep == pl.num_programs(1) - 1)
    def _():
        out_ref[...] = h_ref[...] + _rms_scale(out_ref[...], post_ref[...])


def _mlp(h, pre_w, w_up, w_down, post_w, *, tm=1024, tk=512):
    m, d = h.shape
    dff = w_up.shape[1]
    declared = (4 * _nbytes((tm, d), F32) + 2 * _nbytes((d, tk), BF16) + 2 * _nbytes((tk, d), BF16)
                + _nbytes((tm, d), BF16) + 2 * _nbytes((tm, tk), F32))
    return pl.pallas_call(
        _mlp_kernel,
        grid=(m // tm, dff // tk),
        in_specs=[
            pl.BlockSpec((tm, d), lambda i, k: (i, 0)),
            pl.BlockSpec((1, d), lambda i, k: (0, 0)),
            pl.BlockSpec((d, tk), lambda i, k: (0, k)),
            pl.BlockSpec((tk, d), lambda i, k: (k, 0)),
            pl.BlockSpec((1, d), lambda i, k: (0, 0)),
        ],
        out_specs=pl.BlockSpec((tm, d), lambda i, k: (i, 0)),
        out_shape=jax.ShapeDtypeStruct((m, d), F32),
        scratch_shapes=[pltpu.VMEM((tm, d), BF16)],
        compiler_params=pltpu.CompilerParams(
            dimension_semantics=("arbitrary", "arbitrary"),
            vmem_limit_bytes=_vmem_limit(declared)),
        name="mlp",
    )(h, pre_w.reshape(1, d), w_up, w_down, post_w.reshape(1, d))


def kernel(x, norm_mix_pre, norm_mix_post, norm_mlp_pre, norm_mlp_post, hgrn_w_in, hgrn_lb_logits, hgrn_norm, hgrn_w_out, gla_w_in, gla_w_gk, gla_b_gk, gla_norm, gla_w_out, mlp_w_up, mlp_w_down):
    batch, seq, d = x.shape
    depth = norm_mix_pre.shape[0]
    h = x.reshape(batch * seq, d)
    for layer in range(depth):
        j = layer // 2
        if layer % 2 == 0:
            gain_row = jnp.tile(hgrn_norm[j], HGRN_HEADS).reshape(1, -1)
            q, k, g1, g2, g3, v, gate = _hgrn_inproj(h, norm_mix_pre[layer], hgrn_w_in[j].astype(BF16),
                                                     hgrn_lb_logits, gain_row, layer)
            o = _gla_chunk(q, k, (g1, g2, g3), v, gate, batch=batch, heads=HGRN_HEADS,
                           group=8, rows_per_step=512)
            w_out = hgrn_w_out[j]
        else:
            key_dim = gla_w_gk.shape[2]
            rank = gla_w_gk.shape[1]
            value_dim = (gla_w_in.shape[2] - 2 * key_dim - rank) // 2
            n_main = 2 * key_dim + 2 * value_dim
            w_r = jnp.pad(gla_w_in[j, :, n_main:], ((0, 0), (0, LANES - rank))).astype(BF16)
            w_gk = jnp.pad(gla_w_gk[j], ((0, LANES - rank), (0, 0)))
            gain_row = jnp.tile(gla_norm[j], GLA_HEADS).reshape(1, -1)
            q, k, g1, g2, g3, v, gate = _gla_inproj(h, norm_mix_pre[layer], gla_w_in[j].astype(BF16), w_r,
                                                    w_gk, gla_b_gk[j], gain_row,
                                                    key_dim=key_dim, value_dim=value_dim, heads=GLA_HEADS)
            o = _gla_chunk(q, k, (g1, g2, g3), v, gate, batch=batch, heads=GLA_HEADS,
                           group=4, rows_per_step=512)
            w_out = gla_w_out[j]
        h = _outproj(o, w_out.astype(BF16), h, norm_mix_post[layer])
        h = _mlp(h, norm_mlp_pre[layer], mlp_w_up[layer].astype(BF16), mlp_w_down[layer].astype(BF16),
                 norm_mlp_post[layer])
    return h.reshape(batch, seq, d)
```

```python
import functools
import math

import numpy as np
import jax
import jax.numpy as jnp
from jax import lax
from jax.experimental import pallas as pl
from jax.experimental.pallas import tpu as pltpu

F32 = jnp.float32
BF16 = jnp.bfloat16

NORM_EPS = 1e-6
CHUNK = 64
HGRN_HEADS = 16
GLA_HEADS = 4
GLA_GATE_NORMALIZER = 16.0
LOG2_E = math.log2(math.e)

V7X_VMEM_BYTES = 64 * 1024 * 1024
VMEM_HEADROOM_BYTES = 6 * 1024 * 1024
LANES = 128
MLP_DOWN_COLS = 512


def _vmem_limit(declared_bytes):
    return int(min(V7X_VMEM_BYTES - 2 * 1024 * 1024, declared_bytes + VMEM_HEADROOM_BYTES))


def _nbytes(shape, dtype):
    return int(np.prod(shape)) * jnp.dtype(dtype).itemsize


def _dot(a, b):
    return jnp.dot(a, b, preferred_element_type=F32)


def _dot_nt(a, b):
    return lax.dot_general(a, b, (((1,), (1,)), ((), ())), preferred_element_type=F32)


def _dot_tn(a, b):
    return lax.dot_general(a, b, (((0,), (0,)), ((), ())), preferred_element_type=F32)


def _sigmoid(x):
    return 1.0 / (1.0 + jnp.exp(-x))


def _silu(x):
    return x * _sigmoid(x)


def _rms_scale(x, w):
    ms = jnp.mean(x * x, axis=-1, keepdims=True)
    return x * lax.rsqrt(ms + NORM_EPS) * w


def _split3_bf16(x):
    a = x.astype(BF16)
    r = x - a.astype(F32)
    b = r.astype(BF16)
    return a, b, (r - b.astype(F32)).astype(BF16)


def _hgrn_inproj_kernel(x_ref, nw_ref, wq_ref, wf_ref, wi_ref, wg_ref, lbl_ref, gain_ref,
                        q_ref, k_ref, g1_ref, g2_ref, g3_ref, v_ref, gt_ref, xn_ref, *, layer):
    @pl.when(pl.program_id(1) == 0)
    def _():
        xn_ref[...] = _rms_scale(x_ref[...], nw_ref[...]).astype(BF16)

    xn = xn_ref[...]
    q_ref[...] = _silu(_dot(xn, wq_ref[...]))
    lg = lbl_ref[...]
    e = jnp.exp(lg - jnp.max(lg, axis=0, keepdims=True))
    lb = jnp.sum(e[:layer + 1], axis=0, keepdims=True) / jnp.sum(e, axis=0, keepdims=True)
    forget = lb + (1.0 - lb) * _sigmoid(_dot(xn, wf_ref[...]))
    k_ref[...] = 1.0 - forget
    g1_ref[...], g2_ref[...], g3_ref[...] = _split3_bf16(jnp.log2(forget))
    v_ref[...] = _dot(xn, wi_ref[...]).astype(BF16)
    gt_ref[...] = _silu(_dot(xn, wg_ref[...])) * gain_ref[...]


def _hgrn_inproj(h, norm_w, w_in, lb_logits, gain_row, layer, *, tm=1024, tn=256):
    m, d = h.shape
    df = w_in.shape[1] // 4
    tiles = df // tn
    kern = functools.partial(_hgrn_inproj_kernel, layer=layer)
    bf_out = jax.ShapeDtypeStruct((m, df), BF16)
    f32_out = jax.ShapeDtypeStruct((m, df), F32)
    declared = (2 * _nbytes((tm, d), F32) + _nbytes((tm, d), BF16) + 8 * _nbytes((d, tn), BF16)
                + 2 * (4 * _nbytes((tm, tn), BF16) + 3 * _nbytes((tm, tn), F32)))
    w_spec = [pl.BlockSpec((d, tn), functools.partial(lambda i, j, g: (0, g * tiles + j), g=g))
              for g in range(4)]
    out_spec = pl.BlockSpec((tm, tn), lambda i, j: (i, j))
    return pl.pallas_call(
        kern,
        grid=(m // tm, tiles),
        in_specs=[
            pl.BlockSpec((tm, d), lambda i, j: (i, 0)),
            pl.BlockSpec((1, d), lambda i, j: (0, 0)),
            *w_spec,
            pl.BlockSpec((lb_logits.shape[0], tn), lambda i, j: (0, j)),
            pl.BlockSpec((1, tn), lambda i, j: (0, j)),
        ],
        out_specs=[out_spec] * 7,
        out_shape=[f32_out] * 2 + [bf_out] * 4 + [f32_out],
        scratch_shapes=[pltpu.VMEM((tm, d), BF16)],
        compiler_params=pltpu.CompilerParams(
            dimension_semantics=("arbitrary", "arbitrary"),
            vmem_limit_bytes=_vmem_limit(declared)),
        name="hgrn_inproj",
    )(h, norm_w.reshape(1, d), w_in, w_in, w_in, w_in, lb_logits, gain_row)


def _gla_inproj_kernel(x_ref, nw_ref, wq_ref, wk_ref, wv_ref, wg_ref, wr_ref, wgk_ref, bgk_ref,
                       gain_ref, q_ref, k_ref, g1_ref, g2_ref, g3_ref, v_ref, gt_ref,
                       xn_ref, r_ref, *, q_scale):
    @pl.when(pl.program_id(1) == 0)
    def _():
        xn0 = _rms_scale(x_ref[...], nw_ref[...]).astype(BF16)
        xn_ref[...] = xn0
        r_ref[...] = _dot(xn0, wr_ref[...])

    xn = xn_ref[...]
    q_ref[...] = _dot(xn, wq_ref[...]) * q_scale
    k_ref[...] = _dot(xn, wk_ref[...])
    v_ref[...] = _dot(xn, wv_ref[...]).astype(BF16)
    gt_ref[...] = _silu(_dot(xn, wg_ref[...])) * gain_ref[...]
    r = r_ref[...]
    w = wgk_ref[...]
    r_hi = r.astype(BF16)
    r_lo = (r - r_hi.astype(F32)).astype(BF16)
    w_hi = w.astype(BF16)
    w_lo = (w - w_hi.astype(F32)).astype(BF16)
    z = _dot(r_hi, w_hi) + (_dot(r_hi, w_lo) + _dot(r_lo, w_hi)) + bgk_ref[...]
    log_sig = jnp.minimum(z, 0.0) - jnp.log(1.0 + jnp.exp(-jnp.abs(z)))
    g1_ref[...], g2_ref[...], g3_ref[...] = _split3_bf16(log_sig * (LOG2_E / GLA_GATE_NORMALIZER))


def _gla_inproj(h, norm_w, w_in, w_r, w_gk, b_gk, gain_row, *, key_dim, value_dim, heads,
                tm=1024, steps=4):
    m, d = h.shape
    tk, tv = key_dim // steps, value_dim // steps
    kern = functools.partial(_gla_inproj_kernel, q_scale=float((key_dim // heads) ** -0.5))
    declared = (2 * _nbytes((tm, d), F32) + _nbytes((tm, d), BF16)
                + 4 * _nbytes((d, tk), BF16) + 4 * _nbytes((d, tv), BF16)
                + 2 * _nbytes((d, LANES), BF16) + _nbytes((tm, LANES), F32)
                + 2 * (3 * _nbytes((tm, tk), BF16) + 2 * _nbytes((tm, tk), F32)
                       + _nbytes((tm, tv), F32) + _nbytes((tm, tv), BF16)))
    k_first, v_first, g_first = key_dim // tk, 2 * key_dim // tv, (2 * key_dim + value_dim) // tv
    key_out = pl.BlockSpec((tm, tk), lambda i, j: (i, j))
    val_out = pl.BlockSpec((tm, tv), lambda i, j: (i, j))
    key_bf = jax.ShapeDtypeStruct((m, key_dim), BF16)
    key_f32 = jax.ShapeDtypeStruct((m, key_dim), F32)
    return pl.pallas_call(
        kern,
        grid=(m // tm, steps),
        in_specs=[
            pl.BlockSpec((tm, d), lambda i, j: (i, 0)),
            pl.BlockSpec((1, d), lambda i, j: (0, 0)),
            pl.BlockSpec((d, tk), lambda i, j: (0, j)),
            pl.BlockSpec((d, tk), lambda i, j: (0, k_first + j)),
            pl.BlockSpec((d, tv), lambda i, j: (0, v_first + j)),
            pl.BlockSpec((d, tv), lambda i, j: (0, g_first + j)),
            pl.BlockSpec((d, LANES), lambda i, j: (0, 0)),
            pl.BlockSpec((LANES, tk), lambda i, j: (0, j)),
            pl.BlockSpec((1, tk), lambda i, j: (0, j)),
            pl.BlockSpec((1, tv), lambda i, j: (0, j)),
        ],
        out_specs=[key_out] * 5 + [val_out, val_out],
        out_shape=[key_f32] * 2 + [key_bf] * 3 + [jax.ShapeDtypeStruct((m, value_dim), BF16),
                                     jax.ShapeDtypeStruct((m, value_dim), F32)],
        scratch_shapes=[pltpu.VMEM((tm, d), BF16), pltpu.VMEM((tm, LANES), F32)],
        compiler_params=pltpu.CompilerParams(
            dimension_semantics=("arbitrary", "arbitrary"),
            vmem_limit_bytes=_vmem_limit(declared)),
        name="gla_inproj",
    )(h, norm_w.reshape(1, d), w_in, w_in, w_in, w_in, w_r, w_gk, b_gk.reshape(1, key_dim), gain_row)


def _pair_levels(c):
    i = np.arange(c)[:, None]
    j = np.arange(c)[None, :]
    x = i ^ j
    lvl = np.where(x > 0, np.floor(np.log2(np.maximum(x, 1))).astype(np.int64) + 1, 0)
    return np.where(j > i, -1, lvl).astype(np.int32)


def _neg_abs(x):
    bits = lax.bitcast_convert_type(x, jnp.int32) | jnp.int32(-2 ** 31)
    return lax.bitcast_convert_type(bits, F32)


def _reference_rows(b, b_ref, row, s, c):
    dk = b.shape[-1]
    if s == 1:
        return jnp.where((row & 1) != 0, pltpu.roll(b, 1, 0), b)
    if s == 2:
        lo = [jnp.broadcast_to(b_ref[pl.ds(8 * v + 1, 1), :], (8, dk)) for v in range(c // 8)]
        hi = [jnp.broadcast_to(b_ref[pl.ds(8 * v + 5, 1), :], (8, dk)) for v in range(c // 8)]
        return jnp.where((row & 4) != 0, jnp.concatenate(hi, axis=0), jnp.concatenate(lo, axis=0))
    parts = [jnp.broadcast_to(b_ref[pl.ds(m * 2 * s + s - 1, 1), :], (2 * s, dk))
             for m in range(c // (2 * s))]
    return parts[0] if len(parts) == 1 else jnp.concatenate(parts, axis=0)


def _gla_chunk_kernel(lvl_ref, q_ref, k_ref, g1_ref, g2_ref, g3_ref, v_ref, gt_ref, o_ref,
                      st_ref, b0_ref, b1_ref, op_ref, *, chunk, n_chunks, group):
    c = chunk
    dk = q_ref.shape[-1] // group
    dv = v_ref.shape[-1] // group

    @pl.when(pl.program_id(2) == 0)
    def _():
        st_ref[...] = jnp.zeros_like(st_ref)
        op_ref[...] = jnp.zeros_like(op_ref)

    lvl = lvl_ref[...]
    tril = jnp.where(lvl >= 0, 1.0, 0.0).astype(BF16)
    row = lax.broadcasted_iota(jnp.int32, (c, dk), 0)
    heads = range(group)
    kcols = [slice(hh * dk, (hh + 1) * dk) for hh in heads]
    vcols = [slice(hh * dv, (hh + 1) * dv) for hh in heads]

    def chunk_rows(ci):
        return pl.ds(pl.multiple_of(ci * c, c), c)

    def cumulative_decay(rows, b_ref):
        for hh in heads:
            b_ref[hh] = (_dot(tril, g1_ref[rows, kcols[hh]])
                         + (_dot(tril, g2_ref[rows, kcols[hh]]) + _dot(tril, g3_ref[rows, kcols[hh]])))

    def finish(rows):
        for hh in heads:
            o = op_ref[hh]
            ms = jnp.mean(o * o, axis=-1, keepdims=True)
            on = o * lax.rsqrt(ms + NORM_EPS) * gt_ref[rows, vcols[hh]]
            o_ref[rows, vcols[hh]] = on.astype(o_ref.dtype)

    def chunk_step(ci, b_cur_ref, b_next_ref):
        rows = chunk_rows(ci)
        cumulative_decay(chunk_rows(jnp.minimum(ci + 1, n_chunks - 1)), b_next_ref)
        finish(chunk_rows(jnp.maximum(ci - 1, 0)))

        for hh in heads:
            q = q_ref[rows, kcols[hh]]
            k = k_ref[rows, kcols[hh]]
            b = b_cur_ref[hh]
            b_last = b_cur_ref[hh, pl.ds(c - 1, 1), :]
            st = st_ref[hh]
            op_ref[hh] = _dot_nt((q * jnp.exp2(b)).astype(BF16), st.astype(BF16))
            kl = (k * jnp.exp2(b_last - b)).astype(BF16)
            st_ref[hh] = st * jnp.exp2(b_last) + _dot_tn(v_ref[rows, vcols[hh]], kl)

        scores = []
        for hh in heads:
            q = q_ref[rows, kcols[hh]]
            k = k_ref[rows, kcols[hh]]
            b = b_cur_ref[hh]
            sc = jnp.where(lvl == 0, _dot_nt(q.astype(BF16), k.astype(BF16)), 0.0)
            s = 1
            level = 1
            while s < c:
                e = jnp.exp2(_neg_abs(b - _reference_rows(b, b_cur_ref.at[hh], row, s, c)))
                p = _dot_nt((q * e).astype(BF16), (k * e).astype(BF16))
                sc = jnp.where(lvl == level, p, sc)
                s *= 2
                level += 1
            scores.append(sc.astype(BF16))
        for hh in heads:
            op_ref[hh] += _dot(scores[hh], v_ref[rows, vcols[hh]])

    def body(pair, carry):
        chunk_step(2 * pair, b0_ref, b1_ref)
        chunk_step(2 * pair + 1, b1_ref, b0_ref)
        return carry

    cumulative_decay(chunk_rows(0), b0_ref)
    lax.fori_loop(0, n_chunks // 2, body, 0)
    finish(chunk_rows(n_chunks - 1))


def _gla_chunk(q, k, g_split, v, gate, *, batch, heads, group, rows_per_step, chunk=CHUNK):
    m, kd = q.shape
    vd = v.shape[1]
    dk, dv = kd // heads, vd // heads
    gk, gv = group * dk, group * dv
    seq = m // batch
    tc = min(rows_per_step, seq)
    steps = seq // tc
    assert tc % (2 * chunk) == 0
    kern = functools.partial(_gla_chunk_kernel, chunk=chunk, n_chunks=tc // chunk, group=group)
    lvl = jnp.asarray(_pair_levels(chunk))

    def rows_map(b, h, t):
        return b * steps + t, h

    declared = (2 * (3 * _nbytes((tc, gk), BF16) + 2 * _nbytes((tc, gk), F32)
                     + 2 * _nbytes((tc, gv), BF16) + _nbytes((tc, gv), F32))
                + _nbytes((group, dv, dk), F32) + 2 * _nbytes((group, chunk, dk), F32)
                + _nbytes((group, chunk, dv), F32))
    key_spec = pl.BlockSpec((tc, gk), rows_map)
    val_spec = pl.BlockSpec((tc, gv), rows_map)
    return pl.pallas_call(
        kern,
        grid=(batch, heads // group, steps),
        in_specs=[pl.BlockSpec((chunk, chunk), lambda b, h, t: (0, 0))] + [key_spec] * 5 + [val_spec] * 2,
        out_specs=val_spec,
        out_shape=jax.ShapeDtypeStruct((m, vd), BF16),
        scratch_shapes=[pltpu.VMEM((group, dv, dk), F32), pltpu.VMEM((group, chunk, dk), F32),
                        pltpu.VMEM((group, chunk, dk), F32), pltpu.VMEM((group, chunk, dv), F32)],
        compiler_params=pltpu.CompilerParams(
            dimension_semantics=("arbitrary", "arbitrary", "arbitrary"),
            vmem_limit_bytes=_vmem_limit(declared)),
        name="gla_chunk",
    )(lvl, q, k, *g_split, v, gate)


def _outproj_kernel(o_ref, w_ref, h_ref, nw_ref, out_ref):
    mixed = _dot(o_ref[...], w_ref[...])
    out_ref[...] = h_ref[...] + _rms_scale(mixed, nw_ref[...])


def _outproj(o, w_out, h, norm_w, *, tm=512):
    m, d = h.shape
    dv = o.shape[1]
    declared = (2 * _nbytes((tm, dv), BF16) + 2 * _nbytes((dv, d), BF16) + 4 * _nbytes((tm, d), F32))
    return pl.pallas_call(
        _outproj_kernel,
        grid=(m // tm,),
        in_specs=[
            pl.BlockSpec((tm, dv), lambda i: (i, 0)),
            pl.BlockSpec((dv, d), lambda i: (0, 0)),
            pl.BlockSpec((tm, d), lambda i: (i, 0)),
            pl.BlockSpec((1, d), lambda i: (0, 0)),
        ],
        out_specs=pl.BlockSpec((tm, d), lambda i: (i, 0)),
        out_shape=jax.ShapeDtypeStruct((m, d), F32),
        compiler_params=pltpu.CompilerParams(
            dimension_semantics=("arbitrary",),
            vmem_limit_bytes=_vmem_limit(declared)),
        name="outproj",
    )(o, w_out, h, norm_w.reshape(1, d))


def _mlp_kernel(h_ref, pre_ref, wu_ref, wd_ref, post_ref, out_ref, xn_ref):
    kstep = pl.program_id(1)

    @pl.when(kstep == 0)
    def _():
        xn_ref[...] = _rms_scale(h_ref[...], pre_ref[...]).astype(BF16)
        out_ref[...] = jnp.zeros_like(out_ref)

    u = jnp.maximum(_dot(xn_ref[...], wu_ref[...]), 0.0)
    u2 = (u * u).astype(BF16)
    for n in range(0, out_ref.shape[1], MLP_DOWN_COLS):
        cols = slice(n, n + MLP_DOWN_COLS)
        out_ref[:, cols] += _dot(u2, wd_ref[:, cols])

    @pl.when(kstep == pl.num_programs(1) - 1)
    def _():
        out_ref[...] = h_ref[...] + _rms_scale(out_ref[...], post_ref[...])


def _mlp(h, pre_w, w_up, w_down, post_w, *, tm=1024, tk=512):
    m, d = h.shape
    dff = w_up.shape[1]
    declared = (4 * _nbytes((tm, d), F32) + 2 * _nbytes((d, tk), BF16) + 2 * _nbytes((tk, d), BF16)
                + _nbytes((tm, d), BF16) + 2 * _nbytes((tm, tk), F32))
    return pl.pallas_call(
        _mlp_kernel,
        grid=(m // tm, dff // tk),
        in_specs=[
            pl.BlockSpec((tm, d), lambda i, k: (i, 0)),
            pl.BlockSpec((1, d), lambda i, k: (0, 0)),
            pl.BlockSpec((d, tk), lambda i, k: (0, k)),
            pl.BlockSpec((tk, d), lambda i, k: (k, 0)),
            pl.BlockSpec((1, d), lambda i, k: (0, 0)),
        ],
        out_specs=pl.BlockSpec((tm, d), lambda i, k: (i, 0)),
        out_shape=jax.ShapeDtypeStruct((m, d), F32),
        scratch_shapes=[pltpu.VMEM((tm, d), BF16)],
        compiler_params=pltpu.CompilerParams(
            dimension_semantics=("arbitrary", "arbitrary"),
            vmem_limit_bytes=_vmem_limit(declared)),
        name="mlp",
    )(h, pre_w.reshape(1, d), w_up, w_down, post_w.reshape(1, d))


def kernel(x, norm_mix_pre, norm_mix_post, norm_mlp_pre, norm_mlp_post, hgrn_w_in, hgrn_lb_logits, hgrn_norm, hgrn_w_out, gla_w_in, gla_w_gk, gla_b_gk, gla_norm, gla_w_out, mlp_w_up, mlp_w_down):
    batch, seq, d = x.shape
    depth = norm_mix_pre.shape[0]
    h = x.reshape(batch * seq, d)
    for layer in range(depth):
        j = layer // 2
        if layer % 2 == 0:
            gain_row = jnp.tile(hgrn_norm[j], HGRN_HEADS).reshape(1, -1)
            q, k, g1, g2, g3, v, gate = _hgrn_inproj(h, norm_mix_pre[layer], hgrn_w_in[j].astype(BF16),
                                                     hgrn_lb_logits, gain_row, layer)
            o = _gla_chunk(q, k, (g1, g2, g3), v, gate, batch=batch, heads=HGRN_HEADS,
                           group=8, rows_per_step=512)
            w_out = hgrn_w_out[j]
        else:
            key_dim = gla_w_gk.shape[2]
            rank = gla_w_gk.shape[1]
            value_dim = (gla_w_in.shape[2] - 2 * key_dim - rank) // 2
            n_main = 2 * key_dim + 2 * value_dim
            w_r = jnp.pad(gla_w_in[j, :, n_main:], ((0, 0), (0, LANES - rank))).astype(BF16)
            w_gk = jnp.pad(gla_w_gk[j], ((0, LANES - rank), (0, 0)))
            gain_row = jnp.tile(gla_norm[j], GLA_HEADS).reshape(1, -1)
            q, k, g1, g2, g3, v, gate = _gla_inproj(h, norm_mix_pre[layer], gla_w_in[j].astype(BF16), w_r,
                                                    w_gk, gla_b_gk[j], gain_row,
                                                    key_dim=key_dim, value_dim=value_dim, heads=GLA_HEADS)
            o = _gla_chunk(q, k, (g1, g2, g3), v, gate, batch=batch, heads=GLA_HEADS,
                           group=4, rows_per_step=512)
            w_out = gla_w_out[j]
        h = _outproj(o, w_out.astype(BF16), h, norm_mix_post[layer])
        h = _mlp(h, norm_mlp_pre[layer], mlp_w_up[layer].astype(BF16), mlp_w_down[layer].astype(BF16),
                 norm_mlp_post[layer])
    return h.reshape(batch, seq, d)
```

```python
import functools
import math

import numpy as np
import jax
import jax.numpy as jnp
from jax import lax
from jax.experimental import pallas as pl
from jax.experimental.pallas import tpu as pltpu

F32 = jnp.float32
BF16 = jnp.bfloat16

NORM_EPS = 1e-6
CHUNK = 64
HGRN_HEADS = 16
GLA_HEADS = 4
GLA_GATE_NORMALIZER = 16.0
LOG2_E = math.log2(math.e)

V7X_VMEM_BYTES = 64 * 1024 * 1024
VMEM_HEADROOM_BYTES = 6 * 1024 * 1024
LANES = 128
SUBLANES = 8
SMALL_LEVELS = tuple(2 ** i for i in range(int(math.log2(SUBLANES))))
MLP_DOWN_COLS = 512

def _vmem_limit(declared_bytes):
    return int(min(V7X_VMEM_BYTES - 2 * 1024 * 1024, declared_bytes + VMEM_HEADROOM_BYTES))


def _nbytes(shape, dtype):
    return int(np.prod(shape)) * jnp.dtype(dtype).itemsize


def _dot(a, b):
    return jnp.dot(a, b, preferred_element_type=F32)


def _dot_nt(a, b):
    return lax.dot_general(a, b, (((1,), (1,)), ((), ())), preferred_element_type=F32)


def _dot_tn(a, b):
    return lax.dot_general(a, b, (((0,), (0,)), ((), ())), preferred_element_type=F32)


def _sigmoid(x):
    return 1.0 / (1.0 + jnp.exp(-x))


def _silu(x):
    return x * _sigmoid(x)


def _rms_scale(x, w):
    ms = jnp.mean(x * x, axis=-1, keepdims=True)
    return x * lax.rsqrt(ms + NORM_EPS) * w


def _split3_bf16(x):
    a = x.astype(BF16)
    r = x - a.astype(F32)
    b = r.astype(BF16)
    return a, b, (r - b.astype(F32)).astype(BF16)


def _hgrn_inproj_kernel(x_ref, nw_ref, wq_ref, wf_ref, wi_ref, wg_ref, lbl_ref, gain_ref,
                        q_ref, k_ref, g1_ref, g2_ref, g3_ref, v_ref, gt_ref, xn_ref, *, layer):
    @pl.when(pl.program_id(1) == 0)
    def _():
        xn_ref[...] = _rms_scale(x_ref[...], nw_ref[...]).astype(BF16)

    xn = xn_ref[...]
    lg = lbl_ref[...]
    e = jnp.exp(lg - jnp.max(lg, axis=0, keepdims=True))
    lb = jnp.sum(e[:layer + 1], axis=0, keepdims=True) / jnp.sum(e, axis=0, keepdims=True)
    forget = lb + (1.0 - lb) * _sigmoid(_dot(xn, wf_ref[...].astype(BF16)))
    k_ref[...] = 1.0 - forget
    g1_ref[...], g2_ref[...], g3_ref[...] = _split3_bf16(jnp.log2(forget))
    gt_ref[...] = _silu(_dot(xn, wg_ref[...].astype(BF16))) * gain_ref[...]
    q_ref[...] = _silu(_dot(xn, wq_ref[...].astype(BF16)))
    v_ref[...] = _dot(xn, wi_ref[...].astype(BF16)).astype(BF16)


def _hgrn_inproj(h, norm_w, w_in, lb_logits, gain_row, layer, *, tm=1024, tn=256):
    m, d = h.shape
    df = w_in.shape[1] // 4
    tiles = df // tn
    kern = functools.partial(_hgrn_inproj_kernel, layer=layer)
    bf_out = jax.ShapeDtypeStruct((m, df), BF16)
    f32_out = jax.ShapeDtypeStruct((m, df), F32)
    declared = (2 * _nbytes((tm, d), F32) + _nbytes((tm, d), BF16) + 8 * _nbytes((d, tn), F32)
                + 2 * (4 * _nbytes((tm, tn), BF16) + 3 * _nbytes((tm, tn), F32)))
    w_spec =[pl.BlockSpec((d, tn), functools.partial(lambda i, j, g: (0, g * tiles + j), g=g))
              for g in range(4)]
    out_spec = pl.BlockSpec((tm, tn), lambda i, j: (i, j))
    return pl.pallas_call(
        kern,
        grid=(m // tm, tiles),
        in_specs=[
            pl.BlockSpec((tm, d), lambda i, j: (i, 0)),
            pl.BlockSpec((1, d), lambda i, j: (0, 0)),
            *w_spec,
            pl.BlockSpec((lb_logits.shape[0], tn), lambda i, j: (0, j)),
            pl.BlockSpec((1, tn), lambda i, j: (0, j)),
        ],
        out_specs=[out_spec] * 7,
        out_shape=[f32_out] * 2 + [bf_out] * 4 + [f32_out],
        scratch_shapes=[pltpu.VMEM((tm, d), BF16)],
        compiler_params=pltpu.CompilerParams(
            dimension_semantics=("arbitrary", "arbitrary"),
            vmem_limit_bytes=_vmem_limit(declared)),
        name="hgrn_inproj",
    )(h, norm_w.reshape(1, d), w_in, w_in, w_in, w_in, lb_logits, gain_row)


def _gla_inproj_kernel(x_ref, nw_ref, wq_ref, wk_ref, wv_ref, wg_ref, wr_ref, wgk_ref, bgk_ref,
                       gain_ref, q_ref, k_ref, g1_ref, g2_ref, g3_ref, v_ref, gt_ref,
                       xn_ref, r_ref, *, q_scale):
    @pl.when(pl.program_id(1) == 0)
    def _():
        xn0 = _rms_scale(x_ref[...], nw_ref[...]).astype(BF16)
        xn_ref[...] = xn0
        r_ref[...] = _dot(xn0, wr_ref[...])

    xn = xn_ref[...]
    r = r_ref[...]
    w = wgk_ref[...]
    r_hi = r.astype(BF16)
    r_lo = (r - r_hi.astype(F32)).astype(BF16)
    w_hi = w.astype(BF16)
    w_lo = (w - w_hi.astype(F32)).astype(BF16)
    z = _dot(r_hi, w_hi) + (_dot(r_hi, w_lo) + _dot(r_lo, w_hi)) + bgk_ref[...]
    log_sig = jnp.minimum(z, 0.0) - jnp.log(1.0 + jnp.exp(-jnp.abs(z)))
    g1_ref[...], g2_ref[...], g3_ref[...] = _split3_bf16(log_sig * (LOG2_E / GLA_GATE_NORMALIZER))
    gt_ref[...] = _silu(_dot(xn, wg_ref[...])) * gain_ref[...]
    q_ref[...] = _dot(xn, wq_ref[...]) * q_scale
    k_ref[...] = _dot(xn, wk_ref[...])
    v_ref[...] = _dot(xn, wv_ref[...]).astype(BF16)


def _gla_inproj(h, norm_w, w_in, w_r, w_gk, b_gk, gain_row, *, key_dim, value_dim, heads,
                tm=1024, steps=4):
    m, d = h.shape
    tk, tv = key_dim // steps, value_dim // steps
    kern = functools.partial(_gla_inproj_kernel, q_scale=float((key_dim // heads) ** -0.5))
    declared = (2 * _nbytes((tm, d), F32) + _nbytes((tm, d), BF16)
                + 4 * _nbytes((d, tk), BF16) + 4 * _nbytes((d, tv), BF16)
                + 2 * _nbytes((d, LANES), BF16) + _nbytes((tm, LANES), F32)
                + 2 * (3 * _nbytes((tm, tk), BF16) + 2 * _nbytes((tm, tk), F32)
                       + _nbytes((tm, tv), F32) + _nbytes((tm, tv), BF16)))
    k_first, v_first, g_first = key_dim // tk, 2 * key_dim // tv, (2 * key_dim + value_dim) // tv
    key_out = pl.BlockSpec((tm, tk), lambda i, j: (i, j))
    val_out = pl.BlockSpec((tm, tv), lambda i, j: (i, j))
    key_bf = jax.ShapeDtypeStruct((m, key_dim), BF16)
    key_f32 = jax.ShapeDtypeStruct((m, key_dim), F32)
    return pl.pallas_call(
        kern,
        grid=(m // tm, steps),
        in_specs=[
            pl.BlockSpec((tm, d), lambda i, j: (i, 0)),
            pl.BlockSpec((1, d), lambda i, j: (0, 0)),
            pl.BlockSpec((d, tk), lambda i, j: (0, j)),
            pl.BlockSpec((d, tk), lambda i, j: (0, k_first + j)),
            pl.BlockSpec((d, tv), lambda i, j: (0, v_first + j)),
            pl.BlockSpec((d, tv), lambda i, j: (0, g_first + j)),
            pl.BlockSpec((d, LANES), lambda i, j: (0, 0)),
            pl.BlockSpec((LANES, tk), lambda i, j: (0, j)),
            pl.BlockSpec((1, tk), lambda i, j: (0, j)),
            pl.BlockSpec((1, tv), lambda i, j: (0, j)),
        ],
        out_specs=[key_out] * 5 + [val_out, val_out],
        out_shape=[key_f32] * 2 + [key_bf] * 3 + [jax.ShapeDtypeStruct((m, value_dim), BF16),
                                     jax.ShapeDtypeStruct((m, value_dim), F32)],
        scratch_shapes=[pltpu.VMEM((tm, d), BF16), pltpu.VMEM((tm, LANES), F32)],
        compiler_params=pltpu.CompilerParams(
            dimension_semantics=("arbitrary", "arbitrary"),
            vmem_limit_bytes=_vmem_limit(declared)),
        name="gla_inproj",
    )(h, norm_w.reshape(1, d), w_in, w_in, w_in, w_in, w_r, w_gk, b_gk.reshape(1, key_dim), gain_row)


def _pair_levels(c):
    i = np.arange(c)[:, None]
    j = np.arange(c)[None, :]
    x = i ^ j
    lvl = np.where(x > 0, np.floor(np.log2(np.maximum(x, 1))).astype(np.int64) + 1, 0)
    return np.where(j > i, -1, lvl).astype(np.int32)


def _row_signs(c, dk):
    r = np.arange(c)[None, :, None]
    s = np.array(SMALL_LEVELS)[:, None, None]
    return np.broadcast_to(np.where(r & s, 1.0, -1.0), (len(SMALL_LEVELS), c, dk)).astype(np.float32)


def _reference_rows(b, b_ref, row, s, c):
    dk = b.shape[-1]
    if s == 1:
        return jnp.where((row & 1) != 0, pltpu.roll(b, 1, 0), b)
    if s == 2:
        lo = [jnp.broadcast_to(b_ref[pl.ds(8 * v + 1, 1), :], (8, dk)) for v in range(c // 8)]
        hi = [jnp.broadcast_to(b_ref[pl.ds(8 * v + 5, 1), :], (8, dk)) for v in range(c // 8)]
        return jnp.where((row & 4) != 0, jnp.concatenate(hi, axis=0), jnp.concatenate(lo, axis=0))
    parts = [jnp.broadcast_to(b_ref[pl.ds(m * 2 * s + s - 1, 1), :], (2 * s, dk))
             for m in range(c // (2 * s))]
    return parts[0] if len(parts) == 1 else jnp.concatenate(parts, axis=0)


def _level_exponent(b, b_ref, sgn_ref, row, s, c):
    dk = b.shape[-1]
    if s in SMALL_LEVELS:
        return (b - _reference_rows(b, b_ref, row, s, c)) * sgn_ref[SMALL_LEVELS.index(s)]
    parts = []
    for m in range(c // (2 * s)):
        lo = m * 2 * s
        r = jnp.broadcast_to(b_ref[pl.ds(lo + s - 1, 1), :], (s, dk))
        parts += [r - b[lo:lo + s], b[lo + s:lo + 2 * s] - r]
    return jnp.concatenate(parts, axis=0)


def _gla_chunk_kernel(lvl_ref, sgn_ref, q_ref, k_ref, g1_ref, g2_ref, g3_ref, v_ref, gt_ref, o_ref,
                      st_ref, b0_ref, b1_ref, op_ref, *, chunk, n_chunks, group):
    c = chunk
    dk = q_ref.shape[-1] // group
    dv = v_ref.shape[-1] // group

    @pl.when(pl.program_id(2) == 0)
    def _():
        st_ref[...] = jnp.zeros_like(st_ref)
        op_ref[...] = jnp.zeros_like(op_ref)

    lvl = lvl_ref[...]
    tril = jnp.where(lvl >= 0, 1.0, 0.0).astype(BF16)
    row = lax.broadcasted_iota(jnp.int32, (c, dk), 0)
    heads = range(group)
    kcols = [slice(hh * dk, (hh + 1) * dk) for hh in heads]
    vcols = [slice(hh * dv, (hh + 1) * dv) for hh in heads]

    def chunk_rows(ci):
        return pl.ds(pl.multiple_of(ci * c, c), c)

    def cumulative_decay(rows, b_ref):
        for hh in heads:
            b_ref[hh] = (_dot(tril, g1_ref[rows, kcols[hh]])
                         + (_dot(tril, g2_ref[rows, kcols[hh]]) + _dot(tril, g3_ref[rows, kcols[hh]])))

    def finish(rows):
        for hh in heads:
            o = op_ref[hh]
            ms = jnp.mean(o * o, axis=-1, keepdims=True)
            on = o * lax.rsqrt(ms + NORM_EPS) * gt_ref[rows, vcols[hh]]
            o_ref[rows, vcols[hh]] = on.astype(o_ref.dtype)

    def chunk_step(ci, b_cur_ref, b_next_ref):
        rows = chunk_rows(ci)
        cumulative_decay(chunk_rows(jnp.minimum(ci + 1, n_chunks - 1)), b_next_ref)
        finish(chunk_rows(jnp.maximum(ci - 1, 0)))

        for hh in heads:
            q = q_ref[rows, kcols[hh]]
            k = k_ref[rows, kcols[hh]]
            b = b_cur_ref[hh]
            b_last = b_cur_ref[hh, pl.ds(c - 1, 1), :]
            st = st_ref[hh]
            op_ref[hh] = _dot_nt((q * jnp.exp2(b)).astype(BF16), st.astype(BF16))
            kl = (k * jnp.exp2(b_last - b)).astype(BF16)
            st_ref[hh] = st * jnp.exp2(b_last) + _dot_tn(v_ref[rows, vcols[hh]], kl)

        scores = []
        for hh in heads:
            q = q_ref[rows, kcols[hh]]
            k = k_ref[rows, kcols[hh]]
            b = b_cur_ref[hh]
            sc = jnp.where(lvl == 0, _dot_nt(q.astype(BF16), k.astype(BF16)), 0.0)
            s = 1
            level = 1
            while s < c:
                e = jnp.exp2(_level_exponent(b, b_cur_ref.at[hh], sgn_ref, row, s, c))
                p = _dot_nt((q * e).astype(BF16), (k * e).astype(BF16))
                sc = jnp.where(lvl == level, p, sc)
                s *= 2
                level += 1
            scores.append(sc.astype(BF16))
        for hh in heads:
            op_ref[hh] += _dot(scores[hh], v_ref[rows, vcols[hh]])

    def body(pair, carry):
        chunk_step(2 * pair, b0_ref, b1_ref)
        chunk_step(2 * pair + 1, b1_ref, b0_ref)
        return carry

    cumulative_decay(chunk_rows(0), b0_ref)
    lax.fori_loop(0, n_chunks // 2, body, 0)
    finish(chunk_rows(n_chunks - 1))


def _gla_chunk(q, k, g_split, v, gate, *, batch, heads, group, rows_per_step, chunk=CHUNK):
    m, kd = q.shape
    vd = v.shape[1]
    dk, dv = kd // heads, vd // heads
    gk, gv = group * dk, group * dv
    seq = m // batch
    tc = min(rows_per_step, seq)
    steps = seq // tc
    assert tc % (2 * chunk) == 0
    kern = functools.partial(_gla_chunk_kernel, chunk=chunk, n_chunks=tc // chunk, group=group)
    lvl = jnp.asarray(_pair_levels(chunk))
    sgn = jnp.asarray(_row_signs(chunk, dk))

    def rows_map(b, h, t):
        return b * steps + t, h

    declared = (2 * (3 * _nbytes((tc, gk), BF16) + 2 * _nbytes((tc, gk), F32)
                     + 2 * _nbytes((tc, gv), BF16) + _nbytes((tc, gv), F32))
                + _nbytes((group, dv, dk), F32) + 2 * _nbytes((group, chunk, dk), F32)
                + _nbytes((group, chunk, dv), F32))
    key_spec = pl.BlockSpec((tc, gk), rows_map)
    val_spec = pl.BlockSpec((tc, gv), rows_map)
    return pl.pallas_call(
        kern,
        grid=(batch, heads // group, steps),
        in_specs=[pl.BlockSpec((chunk, chunk), lambda b, h, t: (0, 0)),
                  pl.BlockSpec(sgn.shape, lambda b, h, t: (0, 0, 0))] + [key_spec] * 5 + [val_spec] * 2,
        out_specs=val_spec,
        out_shape=jax.ShapeDtypeStruct((m, vd), BF16),
        scratch_shapes=[pltpu.VMEM((group, dv, dk), F32), pltpu.VMEM((group, chunk, dk), F32),
                        pltpu.VMEM((group, chunk, dk), F32), pltpu.VMEM((group, chunk, dv), F32)],
        compiler_params=pltpu.CompilerParams(
            dimension_semantics=("arbitrary", "arbitrary", "arbitrary"),
            vmem_limit_bytes=_vmem_limit(declared)),
        name="gla_chunk",
    )(lvl, sgn, q, k, *g_split, v, gate)


def _outproj_kernel(o_ref, w_ref, h_ref, nw_ref, out_ref):
    mixed = _dot(o_ref[...], w_ref[...])
    out_ref[...] = h_ref[...] + _rms_scale(mixed, nw_ref[...])


def _outproj(o, w_out, h, norm_w, *, tm=512):
    m, d = h.shape
    dv = o.shape[1]
    declared = (2 * _nbytes((tm, dv), BF16) + 2 * _nbytes((dv, d), BF16) + 4 * _nbytes((tm, d), F32))
    return pl.pallas_call(
        _outproj_kernel,
        grid=(m // tm,),
        in_specs=[
            pl.BlockSpec((tm, dv), lambda i: (i, 0)),
            pl.BlockSpec((dv, d), lambda i: (0, 0)),
            pl.BlockSpec((tm, d), lambda i: (i, 0)),
            pl.BlockSpec((1, d), lambda i: (0, 0)),
        ],
        out_specs=pl.BlockSpec((tm, d), lambda i: (i, 0)),
        out_shape=jax.ShapeDtypeStruct((m, d), F32),
        compiler_params=pltpu.CompilerParams(
            dimension_semantics=("arbitrary",),
            vmem_limit_bytes=_vmem_limit(declared)),
        name="outproj",
    )(o, w_out, h, norm_w.reshape(1, d))


def _mlp_kernel(h_ref, pre_ref, wu_ref, wd_ref, post_ref, out_ref, xn_ref):
    kstep = pl.program_id(1)

    @pl.when(kstep == 0)
    def _():
        xn_ref[...] = _rms_scale(h_ref[...], pre_ref[...]).astype(BF16)
        out_ref[...] = jnp.zeros_like(out_ref)

    u = jnp.maximum(_dot(xn_ref[...], wu_ref[...]), 0.0)
    u2 = (u * u).astype(BF16)
    for n in range(0, out_ref.shape[1], MLP_DOWN_COLS):
        cols = slice(n, n + MLP_DOWN_COLS)
        out_ref[:, cols] += _dot(u2, wd_ref[:, cols])

    @pl.when(kstep == pl.num_programs(1) - 1)
    def _():
        out_ref[...] = h_ref[...] + _rms_scale(out_ref[...], post_ref[...])


def _mlp(h, pre_w, w_up, w_down, post_w, *, tm=1024, tk=512):
    m, d = h.shape
    dff = w_up.shape[1]
    declared = (4 * _nbytes((tm, d), F32) + 2 * _nbytes((d, tk), BF16) + 2 * _nbytes((tk, d), BF16)
                + _nbytes((tm, d), BF16) + 2 * _nbytes((tm, tk), F32))
    return pl.pallas_call(
        _mlp_kernel,
        grid=(m // tm, dff // tk),
        in_specs=[
            pl.BlockSpec((tm, d), lambda i, k: (i, 0)),
            pl.BlockSpec((1, d), lambda i, k: (0, 0)),
            pl.BlockSpec((d, tk), lambda i, k: (0, k)),
            pl.BlockSpec((tk, d), lambda i, k: (k, 0)),
            pl.BlockSpec((1, d), lambda i, k: (0, 0)),
        ],
        out_specs=pl.BlockSpec((tm, d), lambda i, k: (i, 0)),
        out_shape=jax.ShapeDtypeStruct((m, d), F32),
        scratch_shapes=[pltpu.VMEM((tm, d), BF16)],
        compiler_params=pltpu.CompilerParams(
            dimension_semantics=("arbitrary", "arbitrary"),
            vmem_limit_bytes=_vmem_limit(declared)),
        name="mlp",
    )(h, pre_w.reshape(1, d), w_up, w_down, post_w.reshape(1, d))


def kernel(x, norm_mix_pre, norm_mix_post, norm_mlp_pre, norm_mlp_post, hgrn_w_in, hgrn_lb_logits, hgrn_norm, hgrn_w_out, gla_w_in, gla_w_gk, gla_b_gk, gla_norm, gla_w_out, mlp_w_up, mlp_w_down):
    batch, seq, d = x.shape
    depth = norm_mix_pre.shape[0]
    h = x.reshape(batch * seq, d)
    for layer in range(depth):
        j = layer // 2
        if layer % 2 == 0:
            gain_row = jnp.tile(hgrn_norm[j], HGRN_HEADS).reshape(1, -1)
            q, k, g1, g2, g3, v, gate = _hgrn_inproj(h, norm_mix_pre[layer], hgrn_w_in[j],
                                                     hgrn_lb_logits, gain_row, layer)
            o = _gla_chunk(q, k, (g1, g2, g3), v, gate, batch=batch, heads=HGRN_HEADS,
                           group=8, rows_per_step=1024)
            w_out = hgrn_w_out[j]
        else:
            key_dim = gla_w_gk.shape[2]
            rank = gla_w_gk.shape[1]
            value_dim = (gla_w_in.shape[2] - 2 * key_dim - rank) // 2
            n_main = 2 * key_dim + 2 * value_dim
            w_r = jnp.pad(gla_w_in[j, :, n_main:], ((0, 0), (0, LANES - rank))).astype(BF16)
            w_gk = jnp.pad(gla_w_gk[j], ((0, LANES - rank), (0, 0)))
            gain_row = jnp.tile(gla_norm[j], GLA_HEADS).reshape(1, -1)
            q, k, g1, g2, g3, v, gate = _gla_inproj(h, norm_mix_pre[layer], gla_w_in[j].astype(BF16), w_r,
                                                    w_gk, gla_b_gk[j], gain_row,
                                                    key_dim=key_dim, value_dim=value_dim, heads=GLA_HEADS)
            o = _gla_chunk(q, k, (g1, g2, g3), v, gate, batch=batch, heads=GLA_HEADS,
                           group=4, rows_per_step=512, chunk=128)
            w_out = gla_w_out[j]
        h = _outproj(o, w_out.astype(BF16), h, norm_mix_post[layer])
        h = _mlp(h, norm_mlp_pre[layer], mlp_w_up[layer].astype(BF16), mlp_w_down[layer].astype(BF16),
                 norm_mlp_post[layer])
    return h.reshape(batch, seq, d)
```

```python
import functools
import math

import numpy as np
import jax
import jax.numpy as jnp
from jax import lax
from jax.experimental import pallas as pl
from jax.experimental.pallas import tpu as pltpu

F32 = jnp.float32
BF16 = jnp.bfloat16

NORM_EPS = 1e-6
CHUNK = 64
HGRN_HEADS = 16
GLA_HEADS = 4
GLA_GATE_NORMALIZER = 16.0
LOG2_E = math.log2(math.e)

V7X_VMEM_BYTES = 64 * 1024 * 1024
VMEM_HEADROOM_BYTES = 6 * 1024 * 1024
LANES = 128
SUBLANES = 8
SMALL_LEVELS = tuple(2 ** i for i in range(int(math.log2(SUBLANES))))
MLP_DOWN_COLS = 512

def _vmem_limit(declared_bytes):
    return int(min(V7X_VMEM_BYTES - 2 * 1024 * 1024, declared_bytes + VMEM_HEADROOM_BYTES))


def _nbytes(shape, dtype):
    return int(np.prod(shape)) * jnp.dtype(dtype).itemsize


def _dot(a, b):
    return jnp.dot(a, b, preferred_element_type=F32)


def _dot_nt(a, b):
    return lax.dot_general(a, b, (((1,), (1,)), ((), ())), preferred_element_type=F32)


def _dot_tn(a, b):
    return lax.dot_general(a, b, (((0,), (0,)), ((), ())), preferred_element_type=F32)


def _sigmoid(x):
    return 1.0 / (1.0 + jnp.exp(-x))


def _silu(x):
    return x * _sigmoid(x)


def _rms_scale(x, w):
    ms = jnp.mean(x * x, axis=-1, keepdims=True)
    return x * lax.rsqrt(ms + NORM_EPS) * w


def _split3_bf16(x):
    a = x.astype(BF16)
    r = x - a.astype(F32)
    b = r.astype(BF16)
    return a, b, (r - b.astype(F32)).astype(BF16)


def _hgrn_inproj_kernel(x_ref, nw_ref, wq_ref, wf_ref, wi_ref, wg_ref, lbl_ref, gain_ref,
                        q_ref, k_ref, g1_ref, g2_ref, g3_ref, v_ref, gt_ref, xn_ref, *, layer):
    @pl.when(pl.program_id(1) == 0)
    def _():
        xn_ref[...] = _rms_scale(x_ref[...], nw_ref[...]).astype(BF16)

    xn = xn_ref[...]
    lg = lbl_ref[...]
    e = jnp.exp(lg - jnp.max(lg, axis=0, keepdims=True))
    lb = jnp.sum(e[:layer + 1], axis=0, keepdims=True) / jnp.sum(e, axis=0, keepdims=True)
    forget = lb + (1.0 - lb) * _sigmoid(_dot(xn, wf_ref[...]))
    k_ref[...] = 1.0 - forget
    g1_ref[...], g2_ref[...], g3_ref[...] = _split3_bf16(jnp.log2(forget))
    gt_ref[...] = _silu(_dot(xn, wg_ref[...])) * gain_ref[...]
    q_ref[...] = _silu(_dot(xn, wq_ref[...]))
    v_ref[...] = _dot(xn, wi_ref[...]).astype(BF16)


def _hgrn_inproj(h, norm_w, w_in, lb_logits, gain_row, layer, *, tm=1024, tn=256):
    m, d = h.shape
    df = w_in.shape[1] // 4
    tiles = df // tn
    kern = functools.partial(_hgrn_inproj_kernel, layer=layer)
    bf_out = jax.ShapeDtypeStruct((m, df), BF16)
    f32_out = jax.ShapeDtypeStruct((m, df), F32)
    declared = (2 * _nbytes((tm, d), F32) + _nbytes((tm, d), BF16) + 8 * _nbytes((d, tn), BF16)
                + 2 * (4 * _nbytes((tm, tn), BF16) + 3 * _nbytes((tm, tn), F32)))
    w_spec =[pl.BlockSpec((d, tn), functools.partial(lambda i, j, g: (0, g * tiles + j), g=g))
              for g in range(4)]
    out_spec = pl.BlockSpec((tm, tn), lambda i, j: (i, j))
    return pl.pallas_call(
        kern,
        grid=(m // tm, tiles),
        in_specs=[
            pl.BlockSpec((tm, d), lambda i, j: (i, 0)),
            pl.BlockSpec((1, d), lambda i, j: (0, 0)),
            *w_spec,
            pl.BlockSpec((lb_logits.shape[0], tn), lambda i, j: (0, j)),
            pl.BlockSpec((1, tn), lambda i, j: (0, j)),
        ],
        out_specs=[out_spec] * 7,
        out_shape=[f32_out] * 2 + [bf_out] * 4 + [f32_out],
        scratch_shapes=[pltpu.VMEM((tm, d), BF16)],
        compiler_params=pltpu.CompilerParams(
            dimension_semantics=("arbitrary", "arbitrary"),
            vmem_limit_bytes=_vmem_limit(declared)),
        name="hgrn_inproj",
    )(h, norm_w.reshape(1, d), w_in, w_in, w_in, w_in, lb_logits, gain_row)


def _gla_inproj_kernel(x_ref, nw_ref, wq_ref, wk_ref, wv_ref, wg_ref, wr_ref, wgk_ref, bgk_ref,
                       gain_ref, q_ref, k_ref, g1_ref, g2_ref, g3_ref, v_ref, gt_ref,
                       xn_ref, r_ref, *, q_scale):
    @pl.when(pl.program_id(1) == 0)
    def _():
        xn0 = _rms_scale(x_ref[...], nw_ref[...]).astype(BF16)
        xn_ref[...] = xn0
        r_ref[...] = _dot(xn0, wr_ref[...])

    xn = xn_ref[...]
    r = r_ref[...]
    w = wgk_ref[...]
    r_hi = r.astype(BF16)
    r_lo = (r - r_hi.astype(F32)).astype(BF16)
    w_hi = w.astype(BF16)
    w_lo = (w - w_hi.astype(F32)).astype(BF16)
    z = _dot(r_hi, w_hi) + (_dot(r_hi, w_lo) + _dot(r_lo, w_hi)) + bgk_ref[...]
    log_sig = jnp.minimum(z, 0.0) - jnp.log(1.0 + jnp.exp(-jnp.abs(z)))
    g1_ref[...], g2_ref[...], g3_ref[...] = _split3_bf16(log_sig * (LOG2_E / GLA_GATE_NORMALIZER))
    gt_ref[...] = _silu(_dot(xn, wg_ref[...])) * gain_ref[...]
    q_ref[...] = _dot(xn, wq_ref[...]) * q_scale
    k_ref[...] = _dot(xn, wk_ref[...])
    v_ref[...] = _dot(xn, wv_ref[...]).astype(BF16)


def _gla_inproj(h, norm_w, w_in, w_r, w_gk, b_gk, gain_row, *, key_dim, value_dim, heads,
                tm=1024, steps=4):
    m, d = h.shape
    tk, tv = key_dim // steps, value_dim // steps
    kern = functools.partial(_gla_inproj_kernel, q_scale=float((key_dim // heads) ** -0.5))
    declared = (2 * _nbytes((tm, d), F32) + _nbytes((tm, d), BF16)
                + 4 * _nbytes((d, tk), BF16) + 4 * _nbytes((d, tv), BF16)
                + 2 * _nbytes((d, LANES), BF16) + _nbytes((tm, LANES), F32)
                + 2 * (3 * _nbytes((tm, tk), BF16) + 2 * _nbytes((tm, tk), F32)
                       + _nbytes((tm, tv), F32) + _nbytes((tm, tv), BF16)))
    k_first, v_first, g_first = key_dim // tk, 2 * key_dim // tv, (2 * key_dim + value_dim) // tv
    key_out = pl.BlockSpec((tm, tk), lambda i, j: (i, j))
    val_out = pl.BlockSpec((tm, tv), lambda i, j: (i, j))
    key_bf = jax.ShapeDtypeStruct((m, key_dim), BF16)
    key_f32 = jax.ShapeDtypeStruct((m, key_dim), F32)
    return pl.pallas_call(
        kern,
        grid=(m // tm, steps),
        in_specs=[
            pl.BlockSpec((tm, d), lambda i, j: (i, 0)),
            pl.BlockSpec((1, d), lambda i, j: (0, 0)),
            pl.BlockSpec((d, tk), lambda i, j: (0, j)),
            pl.BlockSpec((d, tk), lambda i, j: (0, k_first + j)),
            pl.BlockSpec((d, tv), lambda i, j: (0, v_first + j)),
            pl.BlockSpec((d, tv), lambda i, j: (0, g_first + j)),
            pl.BlockSpec((d, LANES), lambda i, j: (0, 0)),
            pl.BlockSpec((LANES, tk), lambda i, j: (0, j)),
            pl.BlockSpec((1, tk), lambda i, j: (0, j)),
            pl.BlockSpec((1, tv), lambda i, j: (0, j)),
        ],
        out_specs=[key_out] * 5 + [val_out, val_out],
        out_shape=[key_f32] * 2 + [key_bf] * 3 + [jax.ShapeDtypeStruct((m, value_dim), BF16),
                                     jax.ShapeDtypeStruct((m, value_dim), F32)],
        scratch_shapes=[pltpu.VMEM((tm, d), BF16), pltpu.VMEM((tm, LANES), F32)],
        compiler_params=pltpu.CompilerParams(
            dimension_semantics=("arbitrary", "arbitrary"),
            vmem_limit_bytes=_vmem_limit(declared)),
        name="gla_inproj",
    )(h, norm_w.reshape(1, d), w_in, w_in, w_in, w_in, w_r, w_gk, b_gk.reshape(1, key_dim), gain_row)


def _pair_levels(c):
    i = np.arange(c)[:, None]
    j = np.arange(c)[None, :]
    x = i ^ j
    lvl = np.where(x > 0, np.floor(np.log2(np.maximum(x, 1))).astype(np.int64) + 1, 0)
    return np.where(j > i, -1, lvl).astype(np.int32)


def _row_signs(c, dk):
    r = np.arange(c)[None, :, None]
    s = np.array(SMALL_LEVELS)[:, None, None]
    return np.broadcast_to(np.where(r & s, 1.0, -1.0), (len(SMALL_LEVELS), c, dk)).astype(np.float32)


def _reference_rows(b, b_ref, row, s, c):
    dk = b.shape[-1]
    if s == 1:
        return jnp.where((row & 1) != 0, pltpu.roll(b, 1, 0), b)
    if s == 2:
        lo = [jnp.broadcast_to(b_ref[pl.ds(8 * v + 1, 1), :], (8, dk)) for v in range(c // 8)]
        hi = [jnp.broadcast_to(b_ref[pl.ds(8 * v + 5, 1), :], (8, dk)) for v in range(c // 8)]
        return jnp.where((row & 4) != 0, jnp.concatenate(hi, axis=0), jnp.concatenate(lo, axis=0))
    parts = [jnp.broadcast_to(b_ref[pl.ds(m * 2 * s + s - 1, 1), :], (2 * s, dk))
             for m in range(c // (2 * s))]
    return parts[0] if len(parts) == 1 else jnp.concatenate(parts, axis=0)


def _level_exponent(b, b_ref, sgn_ref, row, s, c):
    dk = b.shape[-1]
    if s in SMALL_LEVELS:
        return (b - _reference_rows(b, b_ref, row, s, c)) * sgn_ref[SMALL_LEVELS.index(s)]
    parts = []
    for m in range(c // (2 * s)):
        lo = m * 2 * s
        r = jnp.broadcast_to(b_ref[pl.ds(lo + s - 1, 1), :], (s, dk))
        parts += [r - b[lo:lo + s], b[lo + s:lo + 2 * s] - r]
    return jnp.concatenate(parts, axis=0)


def _gla_chunk_kernel(*refs, chunk, n_chunks, group, n_cast):
    lvl_ref, sgn_ref, q_ref, k_ref, g1_ref, g2_ref, g3_ref, v_ref, gt_ref = refs[:9]
    cast_src = refs[9:9 + n_cast]
    o_ref = refs[9 + n_cast]
    cast_dst = refs[10 + n_cast:10 + 2 * n_cast]
    st_ref, b0_ref, b1_ref, op_ref = refs[10 + 2 * n_cast:]
    c = chunk

    for src, dst in zip(cast_src, cast_dst):
        dst[...] = src[...].astype(dst.dtype)
    dk = q_ref.shape[-1] // group
    dv = v_ref.shape[-1] // group

    @pl.when(pl.program_id(2) == 0)
    def _():
        st_ref[...] = jnp.zeros_like(st_ref)
        op_ref[...] = jnp.zeros_like(op_ref)

    lvl = lvl_ref[...]
    tril = jnp.where(lvl >= 0, 1.0, 0.0).astype(BF16)
    row = lax.broadcasted_iota(jnp.int32, (c, dk), 0)
    heads = range(group)
    kcols = [slice(hh * dk, (hh + 1) * dk) for hh in heads]
    vcols = [slice(hh * dv, (hh + 1) * dv) for hh in heads]

    def chunk_rows(ci):
        return pl.ds(pl.multiple_of(ci * c, c), c)

    def cumulative_decay(rows, b_ref):
        for hh in heads:
            b_ref[hh] = (_dot(tril, g1_ref[rows, kcols[hh]])
                         + (_dot(tril, g2_ref[rows, kcols[hh]]) + _dot(tril, g3_ref[rows, kcols[hh]])))

    def finish(rows):
        for hh in heads:
            o = op_ref[hh]
            ms = jnp.mean(o * o, axis=-1, keepdims=True)
            on = o * lax.rsqrt(ms + NORM_EPS) * gt_ref[rows, vcols[hh]]
            o_ref[rows, vcols[hh]] = on.astype(o_ref.dtype)

    def chunk_step(ci, b_cur_ref, b_next_ref):
        rows = chunk_rows(ci)
        cumulative_decay(chunk_rows(jnp.minimum(ci + 1, n_chunks - 1)), b_next_ref)
        finish(chunk_rows(jnp.maximum(ci - 1, 0)))

        for hh in heads:
            q = q_ref[rows, kcols[hh]]
            k = k_ref[rows, kcols[hh]]
            b = b_cur_ref[hh]
            b_last = b_cur_ref[hh, pl.ds(c - 1, 1), :]
            st = st_ref[hh]
            op_ref[hh] = _dot_nt((q * jnp.exp2(b)).astype(BF16), st.astype(BF16))
            kl = (k * jnp.exp2(b_last - b)).astype(BF16)
            st_ref[hh] = st * jnp.exp2(b_last) + _dot_tn(v_ref[rows, vcols[hh]], kl)

        scores = []
        for hh in heads:
            q = q_ref[rows, kcols[hh]]
            k = k_ref[rows, kcols[hh]]
            b = b_cur_ref[hh]
            sc = jnp.where(lvl == 0, _dot_nt(q.astype(BF16), k.astype(BF16)), 0.0)
            s = 1
            level = 1
            while s < c:
                e = jnp.exp2(_level_exponent(b, b_cur_ref.at[hh], sgn_ref, row, s, c))
                p = _dot_nt((q * e).astype(BF16), (k * e).astype(BF16))
                sc = jnp.where(lvl == level, p, sc)
                s *= 2
                level += 1
            scores.append(sc.astype(BF16))
        for hh in heads:
            op_ref[hh] += _dot(scores[hh], v_ref[rows, vcols[hh]])

    def body(pair, carry):
        chunk_step(2 * pair, b0_ref, b1_ref)
        chunk_step(2 * pair + 1, b1_ref, b0_ref)
        return carry

    cumulative_decay(chunk_rows(0), b0_ref)
    lax.fori_loop(0, n_chunks // 2, body, 0)
    finish(chunk_rows(n_chunks - 1))


def _gla_chunk(q, k, g_split, v, gate, *, batch, heads, group, rows_per_step, chunk=CHUNK,
               cast_jobs=()):
    m, kd = q.shape
    vd = v.shape[1]
    dk, dv = kd // heads, vd // heads
    gk, gv = group * dk, group * dv
    seq = m // batch
    tc = min(rows_per_step, seq)
    steps = seq // tc
    assert tc % (2 * chunk) == 0
    kern = functools.partial(_gla_chunk_kernel, chunk=chunk, n_chunks=tc // chunk, group=group,
                             n_cast=len(cast_jobs))
    lvl = jnp.asarray(_pair_levels(chunk))
    sgn = jnp.asarray(_row_signs(chunk, dk))

    def rows_map(b, h, t):
        return b * steps + t, h

    declared = (2 * (3 * _nbytes((tc, gk), BF16) + 2 * _nbytes((tc, gk), F32)
                     + 2 * _nbytes((tc, gv), BF16) + _nbytes((tc, gv), F32))
                + _nbytes((group, dv, dk), F32) + 2 * _nbytes((group, chunk, dk), F32)
                + _nbytes((group, chunk, dv), F32))
    key_spec = pl.BlockSpec((tc, gk), rows_map)
    val_spec = pl.BlockSpec((tc, gv), rows_map)

    grid = (batch, heads // group, steps)
    n_steps = batch * (heads // group) * steps

    def flat_step(b, h, t):
        return (b * (heads // group) + h) * steps + t

    cast_in_specs, cast_out_specs, cast_out_shapes = [], [], []
    for w, layer in cast_jobs:
        _, rows, cols = w.shape
        rb = rows // n_steps
        assert rb * n_steps == rows and rb % 16 == 0, (w.shape, n_steps)
        cast_in_specs.append(pl.BlockSpec(
            (None, rb, cols), functools.partial(lambda b, h, t, layer: (layer, flat_step(b, h, t), 0),
                                                layer=layer)))
        cast_out_specs.append(pl.BlockSpec((rb, cols), lambda b, h, t: (flat_step(b, h, t), 0)))
        cast_out_shapes.append(jax.ShapeDtypeStruct((rows, cols), BF16))
        declared += 2 * (_nbytes((rb, cols), F32) + _nbytes((rb, cols), BF16))

    outs = pl.pallas_call(
        kern,
        grid=grid,
        in_specs=[pl.BlockSpec((chunk, chunk), lambda b, h, t: (0, 0)),
                  pl.BlockSpec(sgn.shape, lambda b, h, t: (0, 0, 0))] + [key_spec] * 5 + [val_spec] * 2
                 + cast_in_specs,
        out_specs=[val_spec] + cast_out_specs,
        out_shape=[jax.ShapeDtypeStruct((m, vd), BF16)] + cast_out_shapes,
        scratch_shapes=[pltpu.VMEM((group, dv, dk), F32), pltpu.VMEM((group, chunk, dk), F32),
                        pltpu.VMEM((group, chunk, dk), F32), pltpu.VMEM((group, chunk, dv), F32)],
        compiler_params=pltpu.CompilerParams(
            dimension_semantics=("arbitrary", "arbitrary", "arbitrary"),
            vmem_limit_bytes=_vmem_limit(declared)),
        name="gla_chunk",
    )(lvl, sgn, q, k, *g_split, v, gate, *[w for w, _ in cast_jobs])
    return outs[0], outs[1:]


def _outproj_kernel(o_ref, w_ref, h_ref, nw_ref, out_ref):
    mixed = _dot(o_ref[...], w_ref[...])
    out_ref[...] = h_ref[...] + _rms_scale(mixed, nw_ref[...])


def _outproj(o, w_out, h, norm_w, *, tm=512):
    m, d = h.shape
    dv = o.shape[1]
    declared = (2 * _nbytes((tm, dv), BF16) + 2 * _nbytes((dv, d), BF16) + 4 * _nbytes((tm, d), F32))
    return pl.pallas_call(
        _outproj_kernel,
        grid=(m // tm,),
        in_specs=[
            pl.BlockSpec((tm, dv), lambda i: (i, 0)),
            pl.BlockSpec((dv, d), lambda i: (0, 0)),
            pl.BlockSpec((tm, d), lambda i: (i, 0)),
            pl.BlockSpec((1, d), lambda i: (0, 0)),
        ],
        out_specs=pl.BlockSpec((tm, d), lambda i: (i, 0)),
        out_shape=jax.ShapeDtypeStruct((m, d), F32),
        compiler_params=pltpu.CompilerParams(
            dimension_semantics=("arbitrary",),
            vmem_limit_bytes=_vmem_limit(declared)),
        name="outproj",
    )(o, w_out, h, norm_w.reshape(1, d))


def _mlp_kernel(h_ref, pre_ref, wu_ref, wd_ref, post_ref, out_ref, xn_ref):
    kstep = pl.program_id(1)

    @pl.when(kstep == 0)
    def _():
        xn_ref[...] = _rms_scale(h_ref[...], pre_ref[...]).astype(BF16)
        out_ref[...] = jnp.zeros_like(out_ref)

    u = jnp.maximum(_dot(xn_ref[...], wu_ref[...]), 0.0)
    u2 = (u * u).astype(BF16)
    for n in range(0, out_ref.shape[1], MLP_DOWN_COLS):
        cols = slice(n, n + MLP_DOWN_COLS)
        out_ref[:, cols] += _dot(u2, wd_ref[:, cols])

    @pl.when(kstep == pl.num_programs(1) - 1)
    def _():
        out_ref[...] = h_ref[...] + _rms_scale(out_ref[...], post_ref[...])


def _mlp(h, pre_w, w_up, w_down, post_w, *, tm=1024, tk=512):
    m, d = h.shape
    dff = w_up.shape[1]
    declared = (4 * _nbytes((tm, d), F32) + 2 * _nbytes((d, tk), BF16) + 2 * _nbytes((tk, d), BF16)
                + _nbytes((tm, d), BF16) + 2 * _nbytes((tm, tk), F32))
    return pl.pallas_call(
        _mlp_kernel,
        grid=(m // tm, dff // tk),
        in_specs=[
            pl.BlockSpec((tm, d), lambda i, k: (i, 0)),
            pl.BlockSpec((1, d), lambda i, k: (0, 0)),
            pl.BlockSpec((d, tk), lambda i, k: (0, k)),
            pl.BlockSpec((tk, d), lambda i, k: (k, 0)),
            pl.BlockSpec((1, d), lambda i, k: (0, 0)),
        ],
        out_specs=pl.BlockSpec((tm, d), lambda i, k: (i, 0)),
        out_shape=jax.ShapeDtypeStruct((m, d), F32),
        scratch_shapes=[pltpu.VMEM((tm, d), BF16)],
        compiler_params=pltpu.CompilerParams(
            dimension_semantics=("arbitrary", "arbitrary"),
            vmem_limit_bytes=_vmem_limit(declared)),
        name="mlp",
    )(h, pre_w.reshape(1, d), w_up, w_down, post_w.reshape(1, d))


def kernel(x, norm_mix_pre, norm_mix_post, norm_mlp_pre, norm_mlp_post, hgrn_w_in, hgrn_lb_logits, hgrn_norm, hgrn_w_out, gla_w_in, gla_w_gk, gla_b_gk, gla_norm, gla_w_out, mlp_w_up, mlp_w_down):
    batch, seq, d = x.shape
    depth = norm_mix_pre.shape[0]
    h = x.reshape(batch * seq, d)
    later = {}
    for layer in range(depth):
        j = layer // 2
        if layer % 2 == 0:
            if layer > 0:
                later["hgrn_in", j] = hgrn_w_in
            later["hgrn_out", j] = hgrn_w_out
        else:
            later["gla_in", j] = gla_w_in
            later["gla_out", j] = gla_w_out
        later["mlp_up", layer] = mlp_w_up
        later["mlp_down", layer] = mlp_w_down
    bf16_w = {("hgrn_in", 0): hgrn_w_in[0].astype(BF16)}

    for layer in range(depth):
        j = layer // 2
        cast_jobs = [(w, key[1]) for key, w in later.items()] if layer == 0 else []
        if layer % 2 == 0:
            gain_row = jnp.tile(hgrn_norm[j], HGRN_HEADS).reshape(1, -1)
            q, k, g1, g2, g3, v, gate = _hgrn_inproj(h, norm_mix_pre[layer], bf16_w["hgrn_in", j],
                                                     hgrn_lb_logits, gain_row, layer)
            o, cast = _gla_chunk(q, k, (g1, g2, g3), v, gate, batch=batch, heads=HGRN_HEADS,
                                 group=8, rows_per_step=512, cast_jobs=cast_jobs)
            w_out = ("hgrn_out", j)
        else:
            key_dim = gla_w_gk.shape[2]
            rank = gla_w_gk.shape[1]
            value_dim = (gla_w_in.shape[2] - 2 * key_dim - rank) // 2
            n_main = 2 * key_dim + 2 * value_dim
            w_r = jnp.pad(gla_w_in[j, :, n_main:], ((0, 0), (0, LANES - rank))).astype(BF16)
            w_gk = jnp.pad(gla_w_gk[j], ((0, LANES - rank), (0, 0)))
            gain_row = jnp.tile(gla_norm[j], GLA_HEADS).reshape(1, -1)
            q, k, g1, g2, g3, v, gate = _gla_inproj(h, norm_mix_pre[layer], bf16_w["gla_in", j], w_r,
                                                    w_gk, gla_b_gk[j], gain_row,
                                                    key_dim=key_dim, value_dim=value_dim, heads=GLA_HEADS)
            o, cast = _gla_chunk(q, k, (g1, g2, g3), v, gate, batch=batch, heads=GLA_HEADS,
                                 group=4, rows_per_step=512, cast_jobs=cast_jobs)
            w_out = ("gla_out", j)
        if layer == 0:
            bf16_w.update(zip(later.keys(), cast))
        h = _outproj(o, bf16_w[w_out], h, norm_mix_post[layer])
        h = _mlp(h, norm_mlp_pre[layer], bf16_w["mlp_up", layer], bf16_w["mlp_down", layer],
                 norm_mlp_post[layer])
    return h.reshape(batch, seq, d)
```

```python
import functools
import math

import numpy as np
import jax
import jax.numpy as jnp
from jax import lax
from jax.experimental import pallas as pl
from jax.experimental.pallas import tpu as pltpu

F32 = jnp.float32
BF16 = jnp.bfloat16

NORM_EPS = 1e-6
CHUNK = 64
HGRN_HEADS = 16
GLA_HEADS = 4
GLA_GATE_NORMALIZER = 16.0
LOG2_E = math.log2(math.e)

V7X_VMEM_BYTES = 64 * 1024 * 1024
VMEM_HEADROOM_BYTES = 6 * 1024 * 1024
LANES = 128
SUBLANES = 8
SMALL_LEVELS = tuple(2 ** i for i in range(int(math.log2(SUBLANES))))
MLP_DOWN_COLS = 512
MLP_ROW_SPLIT = 2

def _vmem_limit(declared_bytes):
    return int(min(V7X_VMEM_BYTES - 2 * 1024 * 1024, declared_bytes + VMEM_HEADROOM_BYTES))


def _nbytes(shape, dtype):
    return int(np.prod(shape)) * jnp.dtype(dtype).itemsize


def _dot(a, b):
    return jnp.dot(a, b, preferred_element_type=F32)


def _dot_nt(a, b):
    return lax.dot_general(a, b, (((1,), (1,)), ((), ())), preferred_element_type=F32)


def _dot_tn(a, b):
    return lax.dot_general(a, b, (((0,), (0,)), ((), ())), preferred_element_type=F32)


def _sigmoid(x):
    return 1.0 / (1.0 + jnp.exp(-x))


def _silu(x):
    return x * _sigmoid(x)


def _rms_scale(x, w):
    ms = jnp.mean(x * x, axis=-1, keepdims=True)
    return x * lax.rsqrt(ms + NORM_EPS) * w


def _split3_bf16(x):
    a = x.astype(BF16)
    r = x - a.astype(F32)
    b = r.astype(BF16)
    return a, b, (r - b.astype(F32)).astype(BF16)


def _hgrn_inproj_kernel(x_ref, nw_ref, wq_ref, wf_ref, wi_ref, wg_ref, lbl_ref, gain_ref,
                        q_ref, k_ref, g1_ref, g2_ref, g3_ref, v_ref, gt_ref, xn_ref, *, layer):
    @pl.when(pl.program_id(1) == 0)
    def _():
        xn_ref[...] = _rms_scale(x_ref[...], nw_ref[...]).astype(BF16)

    xn = xn_ref[...]
    lg = lbl_ref[...]
    e = jnp.exp(lg - jnp.max(lg, axis=0, keepdims=True))
    lb = jnp.sum(e[:layer + 1], axis=0, keepdims=True) / jnp.sum(e, axis=0, keepdims=True)
    forget = lb + (1.0 - lb) * _sigmoid(_dot(xn, wf_ref[...]))
    k_ref[...] = 1.0 - forget
    g1_ref[...], g2_ref[...], g3_ref[...] = _split3_bf16(jnp.log2(forget))
    gt_ref[...] = _silu(_dot(xn, wg_ref[...])) * gain_ref[...]
    q_ref[...] = _silu(_dot(xn, wq_ref[...]))
    v_ref[...] = _dot(xn, wi_ref[...]).astype(BF16)


def _hgrn_inproj(h, norm_w, w_in, lb_logits, gain_row, layer, *, tm=1024, tn=256):
    m, d = h.shape
    df = w_in.shape[1] // 4
    tiles = df // tn
    kern = functools.partial(_hgrn_inproj_kernel, layer=layer)
    bf_out = jax.ShapeDtypeStruct((m, df), BF16)
    f32_out = jax.ShapeDtypeStruct((m, df), F32)
    declared = (2 * _nbytes((tm, d), F32) + _nbytes((tm, d), BF16) + 8 * _nbytes((d, tn), BF16)
                + 2 * (4 * _nbytes((tm, tn), BF16) + 3 * _nbytes((tm, tn), F32)))
    w_spec =[pl.BlockSpec((d, tn), functools.partial(lambda i, j, g: (0, g * tiles + j), g=g))
              for g in range(4)]
    out_spec = pl.BlockSpec((tm, tn), lambda i, j: (i, j))
    return pl.pallas_call(
        kern,
        grid=(m // tm, tiles),
        in_specs=[
            pl.BlockSpec((tm, d), lambda i, j: (i, 0)),
            pl.BlockSpec((1, d), lambda i, j: (0, 0)),
            *w_spec,
            pl.BlockSpec((lb_logits.shape[0], tn), lambda i, j: (0, j)),
            pl.BlockSpec((1, tn), lambda i, j: (0, j)),
        ],
        out_specs=[out_spec] * 7,
        out_shape=[f32_out] * 2 + [bf_out] * 4 + [f32_out],
        scratch_shapes=[pltpu.VMEM((tm, d), BF16)],
        compiler_params=pltpu.CompilerParams(
            dimension_semantics=("arbitrary", "arbitrary"),
            vmem_limit_bytes=_vmem_limit(declared)),
        name="hgrn_inproj",
    )(h, norm_w.reshape(1, d), w_in, w_in, w_in, w_in, lb_logits, gain_row)


def _gla_inproj_kernel(x_ref, nw_ref, wq_ref, wk_ref, wv_ref, wg_ref, wr_ref, wgk_ref, bgk_ref,
                       gain_ref, q_ref, k_ref, g1_ref, g2_ref, g3_ref, v_ref, gt_ref,
                       xn_ref, r_ref, *, q_scale):
    @pl.when(pl.program_id(1) == 0)
    def _():
        xn0 = _rms_scale(x_ref[...], nw_ref[...]).astype(BF16)
        xn_ref[...] = xn0
        r_ref[...] = _dot(xn0, wr_ref[...])

    xn = xn_ref[...]
    r = r_ref[...]
    w = wgk_ref[...]
    r_hi = r.astype(BF16)
    r_lo = (r - r_hi.astype(F32)).astype(BF16)
    w_hi = w.astype(BF16)
    w_lo = (w - w_hi.astype(F32)).astype(BF16)
    z = _dot(r_hi, w_hi) + (_dot(r_hi, w_lo) + _dot(r_lo, w_hi)) + bgk_ref[...]
    log_sig = jnp.minimum(z, 0.0) - jnp.log(1.0 + jnp.exp(-jnp.abs(z)))
    g1_ref[...], g2_ref[...], g3_ref[...] = _split3_bf16(log_sig * (LOG2_E / GLA_GATE_NORMALIZER))
    gt_ref[...] = _silu(_dot(xn, wg_ref[...])) * gain_ref[...]
    q_ref[...] = _dot(xn, wq_ref[...]) * q_scale
    k_ref[...] = _dot(xn, wk_ref[...])
    v_ref[...] = _dot(xn, wv_ref[...]).astype(BF16)


def _gla_inproj(h, norm_w, w_in, w_r, w_gk, b_gk, gain_row, *, key_dim, value_dim, heads,
                tm=1024, steps=4):
    m, d = h.shape
    tk, tv = key_dim // steps, value_dim // steps
    kern = functools.partial(_gla_inproj_kernel, q_scale=float((key_dim // heads) ** -0.5))
    declared = (2 * _nbytes((tm, d), F32) + _nbytes((tm, d), BF16)
                + 4 * _nbytes((d, tk), BF16) + 4 * _nbytes((d, tv), BF16)
                + 2 * _nbytes((d, LANES), BF16) + _nbytes((tm, LANES), F32)
                + 2 * (3 * _nbytes((tm, tk), BF16) + 2 * _nbytes((tm, tk), F32)
                       + _nbytes((tm, tv), F32) + _nbytes((tm, tv), BF16)))
    k_first, v_first, g_first = key_dim // tk, 2 * key_dim // tv, (2 * key_dim + value_dim) // tv
    key_out = pl.BlockSpec((tm, tk), lambda i, j: (i, j))
    val_out = pl.BlockSpec((tm, tv), lambda i, j: (i, j))
    key_bf = jax.ShapeDtypeStruct((m, key_dim), BF16)
    key_f32 = jax.ShapeDtypeStruct((m, key_dim), F32)
    return pl.pallas_call(
        kern,
        grid=(m // tm, steps),
        in_specs=[
            pl.BlockSpec((tm, d), lambda i, j: (i, 0)),
            pl.BlockSpec((1, d), lambda i, j: (0, 0)),
            pl.BlockSpec((d, tk), lambda i, j: (0, j)),
            pl.BlockSpec((d, tk), lambda i, j: (0, k_first + j)),
            pl.BlockSpec((d, tv), lambda i, j: (0, v_first + j)),
            pl.BlockSpec((d, tv), lambda i, j: (0, g_first + j)),
            pl.BlockSpec((d, LANES), lambda i, j: (0, 0)),
            pl.BlockSpec((LANES, tk), lambda i, j: (0, j)),
            pl.BlockSpec((1, tk), lambda i, j: (0, j)),
            pl.BlockSpec((1, tv), lambda i, j: (0, j)),
        ],
        out_specs=[key_out] * 5 + [val_out, val_out],
        out_shape=[key_f32] * 2 + [key_bf] * 3 + [jax.ShapeDtypeStruct((m, value_dim), BF16),
                                     jax.ShapeDtypeStruct((m, value_dim), F32)],
        scratch_shapes=[pltpu.VMEM((tm, d), BF16), pltpu.VMEM((tm, LANES), F32)],
        compiler_params=pltpu.CompilerParams(
            dimension_semantics=("arbitrary", "arbitrary"),
            vmem_limit_bytes=_vmem_limit(declared)),
        name="gla_inproj",
    )(h, norm_w.reshape(1, d), w_in, w_in, w_in, w_in, w_r, w_gk, b_gk.reshape(1, key_dim), gain_row)


def _pair_levels(c):
    i = np.arange(c)[:, None]
    j = np.arange(c)[None, :]
    x = i ^ j
    lvl = np.where(x > 0, np.floor(np.log2(np.maximum(x, 1))).astype(np.int64) + 1, 0)
    return np.where(j > i, -1, lvl).astype(np.int32)


def _row_signs(c, dk):
    r = np.arange(c)[None, :, None]
    s = np.array(SMALL_LEVELS)[:, None, None]
    return np.broadcast_to(np.where(r & s, 1.0, -1.0), (len(SMALL_LEVELS), c, dk)).astype(np.float32)


def _reference_rows(b, b_ref, row, s, c):
    dk = b.shape[-1]
    if s == 1:
        return jnp.where((row & 1) != 0, pltpu.roll(b, 1, 0), b)
    if s == 2:
        lo = [jnp.broadcast_to(b_ref[pl.ds(8 * v + 1, 1), :], (8, dk)) for v in range(c // 8)]
        hi = [jnp.broadcast_to(b_ref[pl.ds(8 * v + 5, 1), :], (8, dk)) for v in range(c // 8)]
        return jnp.where((row & 4) != 0, jnp.concatenate(hi, axis=0), jnp.concatenate(lo, axis=0))
    parts = [jnp.broadcast_to(b_ref[pl.ds(m * 2 * s + s - 1, 1), :], (2 * s, dk))
             for m in range(c // (2 * s))]
    return parts[0] if len(parts) == 1 else jnp.concatenate(parts, axis=0)


def _level_exponent(b, b_ref, sgn_ref, row, s, c):
    dk = b.shape[-1]
    if s in SMALL_LEVELS:
        return (b - _reference_rows(b, b_ref, row, s, c)) * sgn_ref[SMALL_LEVELS.index(s)]
    parts = []
    for m in range(c // (2 * s)):
        lo = m * 2 * s
        r = jnp.broadcast_to(b_ref[pl.ds(lo + s - 1, 1), :], (s, dk))
        parts += [r - b[lo:lo + s], b[lo + s:lo + 2 * s] - r]
    return jnp.concatenate(parts, axis=0)


def _gla_chunk_kernel(*refs, chunk, n_chunks, group, n_cast):
    lvl_ref, sgn_ref, q_ref, k_ref, g1_ref, g2_ref, g3_ref, v_ref, gt_ref = refs[:9]
    cast_src = refs[9:9 + n_cast]
    o_ref = refs[9 + n_cast]
    cast_dst = refs[10 + n_cast:10 + 2 * n_cast]
    st_ref, b0_ref, b1_ref, op_ref = refs[10 + 2 * n_cast:]
    c = chunk

    for src, dst in zip(cast_src, cast_dst):
        dst[...] = src[...].astype(dst.dtype)
    dk = q_ref.shape[-1] // group
    dv = v_ref.shape[-1] // group

    @pl.when(pl.program_id(2) == 0)
    def _():
        st_ref[...] = jnp.zeros_like(st_ref)
        op_ref[...] = jnp.zeros_like(op_ref)

    lvl = lvl_ref[...]
    tril = jnp.where(lvl >= 0, 1.0, 0.0).astype(BF16)
    row = lax.broadcasted_iota(jnp.int32, (c, dk), 0)
    heads = range(group)
    kcols = [slice(hh * dk, (hh + 1) * dk) for hh in heads]
    vcols = [slice(hh * dv, (hh + 1) * dv) for hh in heads]

    def chunk_rows(ci):
        return pl.ds(pl.multiple_of(ci * c, c), c)

    def cumulative_decay(rows, b_ref):
        for hh in heads:
            b_ref[hh] = (_dot(tril, g1_ref[rows, kcols[hh]])
                         + (_dot(tril, g2_ref[rows, kcols[hh]]) + _dot(tril, g3_ref[rows, kcols[hh]])))

    def finish(rows):
        for hh in heads:
            o = op_ref[hh]
            ms = jnp.mean(o * o, axis=-1, keepdims=True)
            on = o * lax.rsqrt(ms + NORM_EPS) * gt_ref[rows, vcols[hh]]
            o_ref[rows, vcols[hh]] = on.astype(o_ref.dtype)

    def chunk_step(ci, b_cur_ref, b_next_ref):
        rows = chunk_rows(ci)
        cumulative_decay(chunk_rows(jnp.minimum(ci + 1, n_chunks - 1)), b_next_ref)
        finish(chunk_rows(jnp.maximum(ci - 1, 0)))

        for hh in heads:
            q = q_ref[rows, kcols[hh]]
            k = k_ref[rows, kcols[hh]]
            b = b_cur_ref[hh]
            b_last = b_cur_ref[hh, pl.ds(c - 1, 1), :]
            st = st_ref[hh]
            op_ref[hh] = _dot_nt((q * jnp.exp2(b)).astype(BF16), st.astype(BF16))
            kl = (k * jnp.exp2(b_last - b)).astype(BF16)
            st_ref[hh] = st * jnp.exp2(b_last) + _dot_tn(v_ref[rows, vcols[hh]], kl)

        scores = []
        for hh in heads:
            q = q_ref[rows, kcols[hh]].astype(BF16)
            k = k_ref[rows, kcols[hh]].astype(BF16)
            b = b_cur_ref[hh]
            sc = jnp.where(lvl == 0, _dot_nt(q, k), 0.0)
            s = 1
            level = 1
            while s < c:
                e = jnp.exp2(_level_exponent(b, b_cur_ref.at[hh], sgn_ref, row, s, c)).astype(BF16)
                p = _dot_nt(q * e, k * e)
                sc = jnp.where(lvl == level, p, sc)
                s *= 2
                level += 1
            scores.append(sc.astype(BF16))
        for hh in heads:
            op_ref[hh] += _dot(scores[hh], v_ref[rows, vcols[hh]])

    def body(pair, carry):
        chunk_step(2 * pair, b0_ref, b1_ref)
        chunk_step(2 * pair + 1, b1_ref, b0_ref)
        return carry

    cumulative_decay(chunk_rows(0), b0_ref)
    lax.fori_loop(0, n_chunks // 2, body, 0)
    finish(chunk_rows(n_chunks - 1))


def _gla_chunk(q, k, g_split, v, gate, *, batch, heads, group, rows_per_step, chunk=CHUNK,
               cast_jobs=()):
    m, kd = q.shape
    vd = v.shape[1]
    dk, dv = kd // heads, vd // heads
    gk, gv = group * dk, group * dv
    seq = m // batch
    tc = min(rows_per_step, seq)
    steps = seq // tc
    assert tc % (2 * chunk) == 0
    kern = functools.partial(_gla_chunk_kernel, chunk=chunk, n_chunks=tc // chunk, group=group,
                             n_cast=len(cast_jobs))
    lvl = jnp.asarray(_pair_levels(chunk))
    sgn = jnp.asarray(_row_signs(chunk, dk))

    def rows_map(b, h, t):
        return b * steps + t, h

    declared = (2 * (3 * _nbytes((tc, gk), BF16) + 2 * _nbytes((tc, gk), F32)
                     + 2 * _nbytes((tc, gv), BF16) + _nbytes((tc, gv), F32))
                + _nbytes((group, dv, dk), F32) + 2 * _nbytes((group, chunk, dk), F32)
                + _nbytes((group, chunk, dv), F32))
    key_spec = pl.BlockSpec((tc, gk), rows_map)
    val_spec = pl.BlockSpec((tc, gv), rows_map)

    grid = (batch, heads // group, steps)
    n_steps = batch * (heads // group) * steps

    def flat_step(b, h, t):
        return (b * (heads // group) + h) * steps + t

    cast_in_specs, cast_out_specs, cast_out_shapes = [], [], []
    for w, layer in cast_jobs:
        _, rows, cols = w.shape
        rb = rows // n_steps
        assert rb * n_steps == rows and rb % 16 == 0, (w.shape, n_steps)
        cast_in_specs.append(pl.BlockSpec(
            (None, rb, cols), functools.partial(lambda b, h, t, layer: (layer, flat_step(b, h, t), 0),
                                                layer=layer)))
        cast_out_specs.append(pl.BlockSpec((rb, cols), lambda b, h, t: (flat_step(b, h, t), 0)))
        cast_out_shapes.append(jax.ShapeDtypeStruct((rows, cols), BF16))
        declared += 2 * (_nbytes((rb, cols), F32) + _nbytes((rb, cols), BF16))

    outs = pl.pallas_call(
        kern,
        grid=grid,
        in_specs=[pl.BlockSpec((chunk, chunk), lambda b, h, t: (0, 0)),
                  pl.BlockSpec(sgn.shape, lambda b, h, t: (0, 0, 0))] + [key_spec] * 5 + [val_spec] * 2
                 + cast_in_specs,
        out_specs=[val_spec] + cast_out_specs,
        out_shape=[jax.ShapeDtypeStruct((m, vd), BF16)] + cast_out_shapes,
        scratch_shapes=[pltpu.VMEM((group, dv, dk), F32), pltpu.VMEM((group, chunk, dk), F32),
                        pltpu.VMEM((group, chunk, dk), F32), pltpu.VMEM((group, chunk, dv), F32)],
        compiler_params=pltpu.CompilerParams(
            dimension_semantics=("arbitrary", "arbitrary", "arbitrary"),
            vmem_limit_bytes=_vmem_limit(declared)),
        name="gla_chunk",
    )(lvl, sgn, q, k, *g_split, v, gate, *[w for w, _ in cast_jobs])
    return outs[0], outs[1:]


def _outproj_kernel(o_ref, w_ref, h_ref, nw_ref, out_ref):
    tm = out_ref.shape[0]
    for r in range(0, tm, tm // 2):
        rows = slice(r, r + tm // 2)
        mixed = _dot(o_ref[rows, :], w_ref[...])
        out_ref[rows, :] = h_ref[rows, :] + _rms_scale(mixed, nw_ref[...])


def _outproj(o, w_out, h, norm_w, *, tm=1024):
    m, d = h.shape
    dv = o.shape[1]
    declared = (2 * _nbytes((tm, dv), BF16) + _nbytes((dv, d), BF16) + 4 * _nbytes((tm, d), F32))
    return pl.pallas_call(
        _outproj_kernel,
        grid=(m // tm,),
        in_specs=[
            pl.BlockSpec((tm, dv), lambda i: (i, 0)),
            pl.BlockSpec((dv, d), lambda i: (0, 0), pipeline_mode=pl.Buffered(1)),
            pl.BlockSpec((tm, d), lambda i: (i, 0)),
            pl.BlockSpec((1, d), lambda i: (0, 0)),
        ],
        out_specs=pl.BlockSpec((tm, d), lambda i: (i, 0)),
        out_shape=jax.ShapeDtypeStruct((m, d), F32),
        compiler_params=pltpu.CompilerParams(
            dimension_semantics=("arbitrary",),
            vmem_limit_bytes=_vmem_limit(declared)),
        name="outproj",
    )(o, w_out, h, norm_w.reshape(1, d))


def _mlp_kernel(h_ref, pre_ref, wu_ref, wd_ref, post_ref, out_ref, xn_ref):
    kstep = pl.program_id(1)
    last = pl.num_programs(1) - 1
    tm = out_ref.shape[0]
    halves = [slice(r, r + tm // MLP_ROW_SPLIT) for r in range(0, tm, tm // MLP_ROW_SPLIT)]

    def hidden(xn):
        u = jnp.maximum(_dot(xn, wu_ref[...]), 0.0)
        return (u * u).astype(BF16)

    def down(u2, rows, add_to):
        for n in range(0, out_ref.shape[1], MLP_DOWN_COLS):
            cols = slice(n, n + MLP_DOWN_COLS)
            part = _dot(u2, wd_ref[:, cols])
            out_ref[rows, cols] = part if add_to is None else add_to[rows, cols] + part

    @pl.when(kstep == 0)
    def _():
        for rows in halves:
            xn = _rms_scale(h_ref[rows, :], pre_ref[...]).astype(BF16)
            xn_ref[rows, :] = xn
            down(hidden(xn), rows, None)

    @pl.when((kstep > 0) & (kstep < last))
    def _():
        down(hidden(xn_ref[...]), slice(None), out_ref)

    @pl.when(kstep == last)
    def _():
        for rows in halves:
            down(hidden(xn_ref[rows, :]), rows, out_ref)
            out_ref[rows, :] = h_ref[rows, :] + _rms_scale(out_ref[rows, :], post_ref[...])


def _mlp(h, pre_w, w_up, w_down, post_w, *, tm=1024, tk=512):
    m, d = h.shape
    dff = w_up.shape[1]
    declared = (4 * _nbytes((tm, d), F32) + 2 * _nbytes((d, tk), BF16) + 2 * _nbytes((tk, d), BF16)
                + _nbytes((tm, d), BF16) + 2 * _nbytes((tm, tk), F32))
    return pl.pallas_call(
        _mlp_kernel,
        grid=(m // tm, dff // tk),
        in_specs=[
            pl.BlockSpec((tm, d), lambda i, k: (i, 0)),
            pl.BlockSpec((1, d), lambda i, k: (0, 0)),
            pl.BlockSpec((d, tk), lambda i, k: (0, k)),
            pl.BlockSpec((tk, d), lambda i, k: (k, 0)),
            pl.BlockSpec((1, d), lambda i, k: (0, 0)),
        ],
        out_specs=pl.BlockSpec((tm, d), lambda i, k: (i, 0)),
        out_shape=jax.ShapeDtypeStruct((m, d), F32),
        scratch_shapes=[pltpu.VMEM((tm, d), BF16)],
        compiler_params=pltpu.CompilerParams(
            dimension_semantics=("arbitrary", "arbitrary"),
            vmem_limit_bytes=_vmem_limit(declared)),
        name="mlp",
    )(h, pre_w.reshape(1, d), w_up, w_down, post_w.reshape(1, d))


def kernel(x, norm_mix_pre, norm_mix_post, norm_mlp_pre, norm_mlp_post, hgrn_w_in, hgrn_lb_logits, hgrn_norm, hgrn_w_out, gla_w_in, gla_w_gk, gla_b_gk, gla_norm, gla_w_out, mlp_w_up, mlp_w_down):
    batch, seq, d = x.shape
    depth = norm_mix_pre.shape[0]
    h = x.reshape(batch * seq, d)
    later = {}
    for layer in range(depth):
        j = layer // 2
        if layer % 2 == 0:
            if layer > 0:
                later["hgrn_in", j] = hgrn_w_in
            later["hgrn_out", j] = hgrn_w_out
        else:
            later["gla_in", j] = gla_w_in
            later["gla_out", j] = gla_w_out
        later["mlp_up", layer] = mlp_w_up
        later["mlp_down", layer] = mlp_w_down
    bf16_w = {("hgrn_in", 0): hgrn_w_in[0].astype(BF16)}

    for layer in range(depth):
        j = layer // 2
        cast_jobs = [(w, key[1]) for key, w in later.items()] if layer == 0 else []
        if layer % 2 == 0:
            gain_row = jnp.tile(hgrn_norm[j], HGRN_HEADS).reshape(1, -1)
            q, k, g1, g2, g3, v, gate = _hgrn_inproj(h, norm_mix_pre[layer], bf16_w["hgrn_in", j],
                                                     hgrn_lb_logits, gain_row, layer)
            o, cast = _gla_chunk(q, k, (g1, g2, g3), v, gate, batch=batch, heads=HGRN_HEADS,
                                 group=8, rows_per_step=512, cast_jobs=cast_jobs)
            w_out = ("hgrn_out", j)
        else:
            key_dim = gla_w_gk.shape[2]
            rank = gla_w_gk.shape[1]
            value_dim = (gla_w_in.shape[2] - 2 * key_dim - rank) // 2
            n_main = 2 * key_dim + 2 * value_dim
            w_r = jnp.pad(gla_w_in[j, :, n_main:], ((0, 0), (0, LANES - rank))).astype(BF16)
            w_gk = jnp.pad(gla_w_gk[j], ((0, LANES - rank), (0, 0)))
            gain_row = jnp.tile(gla_norm[j], GLA_HEADS).reshape(1, -1)
            q, k, g1, g2, g3, v, gate = _gla_inproj(h, norm_mix_pre[layer], bf16_w["gla_in", j], w_r,
                                                    w_gk, gla_b_gk[j], gain_row,
                                                    key_dim=key_dim, value_dim=value_dim, heads=GLA_HEADS)
            o, cast = _gla_chunk(q, k, (g1, g2, g3), v, gate, batch=batch, heads=GLA_HEADS,
                                 group=4, rows_per_step=512, cast_jobs=cast_jobs)
            w_out = ("gla_out", j)
        if layer == 0:
            bf16_w.update(zip(later.keys(), cast))
        h = _outproj(o, bf16_w[w_out], h, norm_mix_post[layer])
        h = _mlp(h, norm_mlp_pre[layer], bf16_w["mlp_up", layer], bf16_w["mlp_down", layer],
                 norm_mlp_post[layer])
    return h.reshape(batch, seq, d)
```

```python
import functools
import math

import numpy as np
import jax
import jax.numpy as jnp
from jax import lax
from jax.experimental import pallas as pl
from jax.experimental.pallas import tpu as pltpu

F32 = jnp.float32
BF16 = jnp.bfloat16

NORM_EPS = 1e-6
CHUNK = 64
HGRN_HEADS = 16
GLA_HEADS = 4
GLA_GATE_NORMALIZER = 16.0
LOG2_E = math.log2(math.e)

V7X_VMEM_BYTES = 64 * 1024 * 1024
VMEM_HEADROOM_BYTES = 6 * 1024 * 1024
LANES = 128
SUBLANES = 8
SMALL_LEVELS = tuple(2 ** i for i in range(int(math.log2(SUBLANES))))
MLP_DOWN_COLS = 512
MLP_ROW_SPLIT = 2

def _vmem_limit(declared_bytes):
    return int(min(V7X_VMEM_BYTES - 2 * 1024 * 1024, declared_bytes + VMEM_HEADROOM_BYTES))


def _nbytes(shape, dtype):
    return int(np.prod(shape)) * jnp.dtype(dtype).itemsize


def _dot(a, b):
    return jnp.dot(a, b, preferred_element_type=F32)


def _dot_nt(a, b):
    return lax.dot_general(a, b, (((1,), (1,)), ((), ())), preferred_element_type=F32)


def _dot_tn(a, b):
    return lax.dot_general(a, b, (((0,), (0,)), ((), ())), preferred_element_type=F32)


def _sigmoid(x):
    return 1.0 / (1.0 + jnp.exp(-x))


def _silu(x):
    return x * _sigmoid(x)


def _rms_scale(x, w):
    ms = jnp.mean(x * x, axis=-1, keepdims=True)
    return x * lax.rsqrt(ms + NORM_EPS) * w


def _split3_bf16(x):
    a = x.astype(BF16)
    r = x - a.astype(F32)
    b = r.astype(BF16)
    return a, b, (r - b.astype(F32)).astype(BF16)


def _hgrn_inproj_kernel(x_ref, nw_ref, wq_ref, wf_ref, wi_ref, wg_ref, lbl_ref, gain_ref,
                        q_ref, k_ref, g1_ref, g2_ref, g3_ref, v_ref, gt_ref, xn_ref, *, layer):
    @pl.when(pl.program_id(1) == 0)
    def _():
        xn_ref[...] = _rms_scale(x_ref[...], nw_ref[...]).astype(BF16)

    xn = xn_ref[...]
    lg = lbl_ref[...]
    e = jnp.exp(lg - jnp.max(lg, axis=0, keepdims=True))
    lb = jnp.sum(e[:layer + 1], axis=0, keepdims=True) / jnp.sum(e, axis=0, keepdims=True)
    forget = lb + (1.0 - lb) * _sigmoid(_dot(xn, wf_ref[...]))
    k_ref[...] = 1.0 - forget
    g1_ref[...], g2_ref[...], g3_ref[...] = _split3_bf16(jnp.log2(forget))
    gt_ref[...] = _silu(_dot(xn, wg_ref[...])) * gain_ref[...]
    q_ref[...] = _silu(_dot(xn, wq_ref[...]))
    v_ref[...] = _dot(xn, wi_ref[...]).astype(BF16)


def _hgrn_inproj(h, norm_w, w_in, lb_logits, gain_row, layer, *, tm=1024, tn=512):
    m, d = h.shape
    df = w_in.shape[1] // 4
    tiles = df // tn
    kern = functools.partial(_hgrn_inproj_kernel, layer=layer)
    bf_out = jax.ShapeDtypeStruct((m, df), BF16)
    f32_out = jax.ShapeDtypeStruct((m, df), F32)
    declared = (2 * _nbytes((tm, d), F32) + _nbytes((tm, d), BF16) + 8 * _nbytes((d, tn), BF16)
                + 2 * (4 * _nbytes((tm, tn), BF16) + 3 * _nbytes((tm, tn), F32)))
    w_spec =[pl.BlockSpec((d, tn), functools.partial(lambda i, j, g: (0, g * tiles + j), g=g))
              for g in range(4)]
    out_spec = pl.BlockSpec((tm, tn), lambda i, j: (i, j))
    return pl.pallas_call(
        kern,
        grid=(m // tm, tiles),
        in_specs=[
            pl.BlockSpec((tm, d), lambda i, j: (i, 0)),
            pl.BlockSpec((1, d), lambda i, j: (0, 0)),
            *w_spec,
            pl.BlockSpec((lb_logits.shape[0], tn), lambda i, j: (0, j)),
            pl.BlockSpec((1, tn), lambda i, j: (0, j)),
        ],
        out_specs=[out_spec] * 7,
        out_shape=[f32_out] * 2 + [bf_out] * 4 + [f32_out],
        scratch_shapes=[pltpu.VMEM((tm, d), BF16)],
        compiler_params=pltpu.CompilerParams(
            dimension_semantics=("arbitrary", "arbitrary"),
            vmem_limit_bytes=_vmem_limit(declared)),
        name="hgrn_inproj",
    )(h, norm_w.reshape(1, d), w_in, w_in, w_in, w_in, lb_logits, gain_row)


def _gla_inproj_kernel(x_ref, nw_ref, wq_ref, wk_ref, wv_ref, wg_ref, wr_ref, wgk_ref, bgk_ref,
                       gain_ref, q_ref, k_ref, g1_ref, g2_ref, g3_ref, v_ref, gt_ref,
                       xn_ref, r_ref, *, q_scale):
    @pl.when(pl.program_id(1) == 0)
    def _():
        xn0 = _rms_scale(x_ref[...], nw_ref[...]).astype(BF16)
        xn_ref[...] = xn0
        r_ref[...] = _dot(xn0, wr_ref[...])

    xn = xn_ref[...]
    r = r_ref[...]
    w = wgk_ref[...]
    r_hi = r.astype(BF16)
    r_lo = (r - r_hi.astype(F32)).astype(BF16)
    w_hi = w.astype(BF16)
    w_lo = (w - w_hi.astype(F32)).astype(BF16)
    z = _dot(r_hi, w_hi) + (_dot(r_hi, w_lo) + _dot(r_lo, w_hi)) + bgk_ref[...]
    log_sig = jnp.minimum(z, 0.0) - jnp.log(1.0 + jnp.exp(-jnp.abs(z)))
    g1_ref[...], g2_ref[...], g3_ref[...] = _split3_bf16(log_sig * (LOG2_E / GLA_GATE_NORMALIZER))
    gt_ref[...] = _silu(_dot(xn, wg_ref[...])) * gain_ref[...]
    q_ref[...] = _dot(xn, wq_ref[...]) * q_scale
    k_ref[...] = _dot(xn, wk_ref[...])
    v_ref[...] = _dot(xn, wv_ref[...]).astype(BF16)


def _gla_inproj(h, norm_w, w_in, w_r, w_gk, b_gk, gain_row, *, key_dim, value_dim, heads,
                tm=1024, steps=4):
    m, d = h.shape
    tk, tv = key_dim // steps, value_dim // steps
    kern = functools.partial(_gla_inproj_kernel, q_scale=float((key_dim // heads) ** -0.5))
    declared = (2 * _nbytes((tm, d), F32) + _nbytes((tm, d), BF16)
                + 4 * _nbytes((d, tk), BF16) + 4 * _nbytes((d, tv), BF16)
                + 2 * _nbytes((d, LANES), BF16) + _nbytes((tm, LANES), F32)
                + 2 * (3 * _nbytes((tm, tk), BF16) + 2 * _nbytes((tm, tk), F32)
                       + _nbytes((tm, tv), F32) + _nbytes((tm, tv), BF16)))
    k_first, v_first, g_first = key_dim // tk, 2 * key_dim // tv, (2 * key_dim + value_dim) // tv
    key_out = pl.BlockSpec((tm, tk), lambda i, j: (i, j))
    val_out = pl.BlockSpec((tm, tv), lambda i, j: (i, j))
    key_bf = jax.ShapeDtypeStruct((m, key_dim), BF16)
    key_f32 = jax.ShapeDtypeStruct((m, key_dim), F32)
    return pl.pallas_call(
        kern,
        grid=(m // tm, steps),
        in_specs=[
            pl.BlockSpec((tm, d), lambda i, j: (i, 0)),
            pl.BlockSpec((1, d), lambda i, j: (0, 0)),
            pl.BlockSpec((d, tk), lambda i, j: (0, j)),
            pl.BlockSpec((d, tk), lambda i, j: (0, k_first + j)),
            pl.BlockSpec((d, tv), lambda i, j: (0, v_first + j)),
            pl.BlockSpec((d, tv), lambda i, j: (0, g_first + j)),
            pl.BlockSpec((d, LANES), lambda i, j: (0, 0)),
            pl.BlockSpec((LANES, tk), lambda i, j: (0, j)),
            pl.BlockSpec((1, tk), lambda i, j: (0, j)),
            pl.BlockSpec((1, tv), lambda i, j: (0, j)),
        ],
        out_specs=[key_out] * 5 + [val_out, val_out],
        out_shape=[key_f32] * 2 + [key_bf] * 3 + [jax.ShapeDtypeStruct((m, value_dim), BF16),
                                     jax.ShapeDtypeStruct((m, value_dim), F32)],
        scratch_shapes=[pltpu.VMEM((tm, d), BF16), pltpu.VMEM((tm, LANES), F32)],
        compiler_params=pltpu.CompilerParams(
            dimension_semantics=("arbitrary", "arbitrary"),
            vmem_limit_bytes=_vmem_limit(declared)),
        name="gla_inproj",
    )(h, norm_w.reshape(1, d), w_in, w_in, w_in, w_in, w_r, w_gk, b_gk.reshape(1, key_dim), gain_row)


def _pair_levels(c):
    i = np.arange(c)[:, None]
    j = np.arange(c)[None, :]
    x = i ^ j
    lvl = np.where(x > 0, np.floor(np.log2(np.maximum(x, 1))).astype(np.int64) + 1, 0)
    return np.where(j > i, -1, lvl).astype(np.int32)


def _row_signs(c, dk):
    r = np.arange(c)[None, :, None]
    s = np.array(SMALL_LEVELS)[:, None, None]
    return np.broadcast_to(np.where(r & s, 1.0, -1.0), (len(SMALL_LEVELS), c, dk)).astype(np.float32)


def _reference_rows(b, b_ref, row, s, c):
    dk = b.shape[-1]
    if s == 1:
        return jnp.where((row & 1) != 0, pltpu.roll(b, 1, 0), b)
    if s == 2:
        lo = [jnp.broadcast_to(b_ref[pl.ds(8 * v + 1, 1), :], (8, dk)) for v in range(c // 8)]
        hi = [jnp.broadcast_to(b_ref[pl.ds(8 * v + 5, 1), :], (8, dk)) for v in range(c // 8)]
        return jnp.where((row & 4) != 0, jnp.concatenate(hi, axis=0), jnp.concatenate(lo, axis=0))
    parts = [jnp.broadcast_to(b_ref[pl.ds(m * 2 * s + s - 1, 1), :], (2 * s, dk))
             for m in range(c // (2 * s))]
    return parts[0] if len(parts) == 1 else jnp.concatenate(parts, axis=0)


def _level_exponent(b, b_ref, sgn_ref, row, s, c):
    dk = b.shape[-1]
    if s in SMALL_LEVELS:
        return (b - _reference_rows(b, b_ref, row, s, c)) * sgn_ref[SMALL_LEVELS.index(s)]
    parts = []
    for m in range(c // (2 * s)):
        lo = m * 2 * s
        r = jnp.broadcast_to(b_ref[pl.ds(lo + s - 1, 1), :], (s, dk))
        parts += [r - b[lo:lo + s], b[lo + s:lo + 2 * s] - r]
    return jnp.concatenate(parts, axis=0)


def _gla_chunk_kernel(*refs, chunk, n_chunks, group, n_cast):
    lvl_ref, sgn_ref, q_ref, k_ref, g1_ref, g2_ref, g3_ref, v_ref, gt_ref = refs[:9]
    cast_src = refs[9:9 + n_cast]
    o_ref = refs[9 + n_cast]
    cast_dst = refs[10 + n_cast:10 + 2 * n_cast]
    st_ref, b0_ref, b1_ref, op_ref = refs[10 + 2 * n_cast:]
    c = chunk

    for src, dst in zip(cast_src, cast_dst):
        dst[...] = src[...].astype(dst.dtype)
    dk = q_ref.shape[-1] // group
    dv = v_ref.shape[-1] // group

    @pl.when(pl.program_id(2) == 0)
    def _():
        st_ref[...] = jnp.zeros_like(st_ref)
        op_ref[...] = jnp.zeros_like(op_ref)

    lvl = lvl_ref[...]
    tril = jnp.where(lvl >= 0, 1.0, 0.0).astype(BF16)
    row = lax.broadcasted_iota(jnp.int32, (c, dk), 0)
    heads = range(group)
    kcols = [slice(hh * dk, (hh + 1) * dk) for hh in heads]
    vcols = [slice(hh * dv, (hh + 1) * dv) for hh in heads]

    def chunk_rows(ci):
        return pl.ds(pl.multiple_of(ci * c, c), c)

    def cumulative_decay(rows, b_ref):
        for hh in heads:
            b_ref[hh] = (_dot(tril, g1_ref[rows, kcols[hh]])
                         + (_dot(tril, g2_ref[rows, kcols[hh]]) + _dot(tril, g3_ref[rows, kcols[hh]])))

    def finish(rows):
        for hh in heads:
            o = op_ref[hh]
            ms = jnp.mean(o * o, axis=-1, keepdims=True)
            on = o * lax.rsqrt(ms + NORM_EPS) * gt_ref[rows, vcols[hh]]
            o_ref[rows, vcols[hh]] = on.astype(o_ref.dtype)

    def chunk_step(ci, b_cur_ref, b_next_ref):
        rows = chunk_rows(ci)
        cumulative_decay(chunk_rows(jnp.minimum(ci + 1, n_chunks - 1)), b_next_ref)
        finish(chunk_rows(jnp.maximum(ci - 1, 0)))

        for hh in heads:
            q = q_ref[rows, kcols[hh]]
            k = k_ref[rows, kcols[hh]]
            b = b_cur_ref[hh]
            b_last = b_cur_ref[hh, pl.ds(c - 1, 1), :]
            st = st_ref[hh]
            op_ref[hh] = _dot_nt((q * jnp.exp2(b)).astype(BF16), st.astype(BF16))
            kl = (k * jnp.exp2(b_last - b)).astype(BF16)
            st_ref[hh] = st * jnp.exp2(b_last) + _dot_tn(v_ref[rows, vcols[hh]], kl)

        scores = []
        for hh in heads:
            q = q_ref[rows, kcols[hh]].astype(BF16)
            k = k_ref[rows, kcols[hh]].astype(BF16)
            b = b_cur_ref[hh]
            sc = jnp.where(lvl == 0, _dot_nt(q, k), 0.0)
            s = 1
            level = 1
            while s < c:
                e = jnp.exp2(_level_exponent(b, b_cur_ref.at[hh], sgn_ref, row, s, c)).astype(BF16)
                p = _dot_nt(q * e, k * e)
                sc = jnp.where(lvl == level, p, sc)
                s *= 2
                level += 1
            scores.append(sc.astype(BF16))
        for hh in heads:
            op_ref[hh] += _dot(scores[hh], v_ref[rows, vcols[hh]])

    def body(pair, carry):
        chunk_step(2 * pair, b0_ref, b1_ref)
        chunk_step(2 * pair + 1, b1_ref, b0_ref)
        return carry

    cumulative_decay(chunk_rows(0), b0_ref)
    lax.fori_loop(0, n_chunks // 2, body, 0)
    finish(chunk_rows(n_chunks - 1))


def _gla_chunk(q, k, g_split, v, gate, *, batch, heads, group, rows_per_step, chunk=CHUNK,
               cast_jobs=()):
    m, kd = q.shape
    vd = v.shape[1]
    dk, dv = kd // heads, vd // heads
    gk, gv = group * dk, group * dv
    seq = m // batch
    tc = min(rows_per_step, seq)
    steps = seq // tc
    assert tc % (2 * chunk) == 0
    kern = functools.partial(_gla_chunk_kernel, chunk=chunk, n_chunks=tc // chunk, group=group,
                             n_cast=len(cast_jobs))
    lvl = jnp.asarray(_pair_levels(chunk))
    sgn = jnp.asarray(_row_signs(chunk, dk))

    def rows_map(b, h, t):
        return b * steps + t, h

    declared = (2 * (3 * _nbytes((tc, gk), BF16) + 2 * _nbytes((tc, gk), F32)
                     + 2 * _nbytes((tc, gv), BF16) + _nbytes((tc, gv), F32))
                + _nbytes((group, dv, dk), F32) + 2 * _nbytes((group, chunk, dk), F32)
                + _nbytes((group, chunk, dv), F32))
    key_spec = pl.BlockSpec((tc, gk), rows_map)
    val_spec = pl.BlockSpec((tc, gv), rows_map)

    grid = (batch, heads // group, steps)
    n_steps = batch * (heads // group) * steps

    def flat_step(b, h, t):
        return (b * (heads // group) + h) * steps + t

    cast_in_specs, cast_out_specs, cast_out_shapes = [], [], []
    for w, layer in cast_jobs:
        _, rows, cols = w.shape
        rb = rows // n_steps
        assert rb * n_steps == rows and rb % 16 == 0, (w.shape, n_steps)
        cast_in_specs.append(pl.BlockSpec(
            (None, rb, cols), functools.partial(lambda b, h, t, layer: (layer, flat_step(b, h, t), 0),
                                                layer=layer)))
        cast_out_specs.append(pl.BlockSpec((rb, cols), lambda b, h, t: (flat_step(b, h, t), 0)))
        cast_out_shapes.append(jax.ShapeDtypeStruct((rows, cols), BF16))
        declared += 2 * (_nbytes((rb, cols), F32) + _nbytes((rb, cols), BF16))

    outs = pl.pallas_call(
        kern,
        grid=grid,
        in_specs=[pl.BlockSpec(lvl.shape, lambda b, h, t: (0, 0)),
                  pl.BlockSpec(sgn.shape, lambda b, h, t: (0, 0, 0))] + [key_spec] * 5 + [val_spec] * 2
                 + cast_in_specs,
        out_specs=[val_spec] + cast_out_specs,
        out_shape=[jax.ShapeDtypeStruct((m, vd), BF16)] + cast_out_shapes,
        scratch_shapes=[pltpu.VMEM((group, dv, dk), F32), pltpu.VMEM((group, chunk, dk), F32),
                        pltpu.VMEM((group, chunk, dk), F32), pltpu.VMEM((group, chunk, dv), F32)],
        compiler_params=pltpu.CompilerParams(
            dimension_semantics=("arbitrary", "arbitrary", "arbitrary"),
            vmem_limit_bytes=_vmem_limit(declared)),
        name="gla_chunk",
    )(lvl, sgn, q, k, *g_split, v, gate, *[w for w, _ in cast_jobs])
    return outs[0], outs[1:]


def _outproj_kernel(o_ref, w_ref, h_ref, nw_ref, out_ref):
    mixed = _dot(o_ref[...], w_ref[...])
    out_ref[...] = h_ref[...] + _rms_scale(mixed, nw_ref[...])


def _outproj(o, w_out, h, norm_w, *, tm=512):
    m, d = h.shape
    dv = o.shape[1]
    declared = (2 * _nbytes((tm, dv), BF16) + 2 * _nbytes((dv, d), BF16) + 4 * _nbytes((tm, d), F32))
    return pl.pallas_call(
        _outproj_kernel,
        grid=(m // tm,),
        in_specs=[
            pl.BlockSpec((tm, dv), lambda i: (i, 0)),
            pl.BlockSpec((dv, d), lambda i: (0, 0)),
            pl.BlockSpec((tm, d), lambda i: (i, 0)),
            pl.BlockSpec((1, d), lambda i: (0, 0)),
        ],
        out_specs=pl.BlockSpec((tm, d), lambda i: (i, 0)),
        out_shape=jax.ShapeDtypeStruct((m, d), F32),
        compiler_params=pltpu.CompilerParams(
            dimension_semantics=("arbitrary",),
            vmem_limit_bytes=_vmem_limit(declared)),
        name="outproj",
    )(o, w_out, h, norm_w.reshape(1, d))


def _mlp_kernel(h_ref, pre_ref, wu_ref, wd_ref, post_ref, out_ref, xn_ref):
    kstep = pl.program_id(1)
    last = pl.num_programs(1) - 1
    tm = out_ref.shape[0]
    halves = [slice(r, r + tm // MLP_ROW_SPLIT) for r in range(0, tm, tm // MLP_ROW_SPLIT)]

    def hidden(xn):
        u = jnp.maximum(_dot(xn, wu_ref[...]), 0.0)
        return (u * u).astype(BF16)

    def down(u2, rows, add_to):
        for n in range(0, out_ref.shape[1], MLP_DOWN_COLS):
            cols = slice(n, n + MLP_DOWN_COLS)
            part = _dot(u2, wd_ref[:, cols])
            out_ref[rows, cols] = part if add_to is None else add_to[rows, cols] + part

    @pl.when(kstep == 0)
    def _():
        for rows in halves:
            xn = _rms_scale(h_ref[rows, :], pre_ref[...]).astype(BF16)
            xn_ref[rows, :] = xn
            down(hidden(xn), rows, None)

    @pl.when((kstep > 0) & (kstep < last))
    def _():
        down(hidden(xn_ref[...]), slice(None), out_ref)

    @pl.when(kstep == last)
    def _():
        for rows in halves:
            down(hidden(xn_ref[rows, :]), rows, out_ref)
            out_ref[rows, :] = h_ref[rows, :] + _rms_scale(out_ref[rows, :], post_ref[...])


def _mlp(h, pre_w, w_up, w_down, post_w, *, tm=1024, tk=1024):
    m, d = h.shape
    dff = w_up.shape[1]
    declared = (4 * _nbytes((tm, d), F32) + 2 * _nbytes((d, tk), BF16) + 2 * _nbytes((tk, d), BF16)
                + _nbytes((tm, d), BF16) + 2 * _nbytes((tm, tk), F32))
    return pl.pallas_call(
        _mlp_kernel,
        grid=(m // tm, dff // tk),
        in_specs=[
            pl.BlockSpec((tm, d), lambda i, k: (i, 0)),
            pl.BlockSpec((1, d), lambda i, k: (0, 0)),
            pl.BlockSpec((d, tk), lambda i, k: (0, k)),
            pl.BlockSpec((tk, d), lambda i, k: (k, 0)),
            pl.BlockSpec((1, d), lambda i, k: (0, 0)),
        ],
        out_specs=pl.BlockSpec((tm, d), lambda i, k: (i, 0)),
        out_shape=jax.ShapeDtypeStruct((m, d), F32),
        scratch_shapes=[pltpu.VMEM((tm, d), BF16)],
        compiler_params=pltpu.CompilerParams(
            dimension_semantics=("arbitrary", "arbitrary"),
            vmem_limit_bytes=_vmem_limit(declared)),
        name="mlp",
    )(h, pre_w.reshape(1, d), w_up, w_down, post_w.reshape(1, d))


def kernel(x, norm_mix_pre, norm_mix_post, norm_mlp_pre, norm_mlp_post, hgrn_w_in, hgrn_lb_logits, hgrn_norm, hgrn_w_out, gla_w_in, gla_w_gk, gla_b_gk, gla_norm, gla_w_out, mlp_w_up, mlp_w_down):
    batch, seq, d = x.shape
    depth = norm_mix_pre.shape[0]
    h = x.reshape(batch * seq, d)
    later = {}
    for layer in range(depth):
        j = layer // 2
        if layer % 2 == 0:
            if layer > 0:
                later["hgrn_in", j] = hgrn_w_in
            later["hgrn_out", j] = hgrn_w_out
        else:
            later["gla_in", j] = gla_w_in
            later["gla_out", j] = gla_w_out
        later["mlp_up", layer] = mlp_w_up
        later["mlp_down", layer] = mlp_w_down
    bf16_w = {("hgrn_in", 0): hgrn_w_in[0].astype(BF16)}

    for layer in range(depth):
        j = layer // 2
        cast_jobs = [(w, key[1]) for key, w in later.items()] if layer == 0 else []
        if layer % 2 == 0:
            gain_row = jnp.tile(hgrn_norm[j], HGRN_HEADS).reshape(1, -1)
            q, k, g1, g2, g3, v, gate = _hgrn_inproj(h, norm_mix_pre[layer], bf16_w["hgrn_in", j],
                                                     hgrn_lb_logits, gain_row, layer)
            o, cast = _gla_chunk(q, k, (g1, g2, g3), v, gate, batch=batch, heads=HGRN_HEADS,
                                 group=16, rows_per_step=256, cast_jobs=cast_jobs)
            w_out = ("hgrn_out", j)
        else:
            key_dim = gla_w_gk.shape[2]
            rank = gla_w_gk.shape[1]
            value_dim = (gla_w_in.shape[2] - 2 * key_dim - rank) // 2
            n_main = 2 * key_dim + 2 * value_dim
            w_r = jnp.pad(gla_w_in[j, :, n_main:], ((0, 0), (0, LANES - rank))).astype(BF16)
            w_gk = jnp.pad(gla_w_gk[j], ((0, LANES - rank), (0, 0)))
            gain_row = jnp.tile(gla_norm[j], GLA_HEADS).reshape(1, -1)
            q, k, g1, g2, g3, v, gate = _gla_inproj(h, norm_mix_pre[layer], bf16_w["gla_in", j], w_r,
                                                    w_gk, gla_b_gk[j], gain_row,
                                                    key_dim=key_dim, value_dim=value_dim, heads=GLA_HEADS)
            o, cast = _gla_chunk(q, k, (g1, g2, g3), v, gate, batch=batch, heads=GLA_HEADS,
                                 group=4, rows_per_step=512, cast_jobs=cast_jobs)
            w_out = ("gla_out", j)
        if layer == 0:
            bf16_w.update(zip(later.keys(), cast))
        h = _outproj(o, bf16_w[w_out], h, norm_mix_post[layer])
        h = _mlp(h, norm_mlp_pre[layer], bf16_w["mlp_up", layer], bf16_w["mlp_down", layer],
                 norm_mlp_post[layer])
    return h.reshape(batch, seq, d)
```

```python
import functools
import math

import numpy as np
import jax
import jax.numpy as jnp
from jax import lax
from jax.experimental import pallas as pl
from jax.experimental.pallas import tpu as pltpu

F32 = jnp.float32
BF16 = jnp.bfloat16

NORM_EPS = 1e-6
CHUNK = 64
HGRN_HEADS = 16
GLA_HEADS = 4
GLA_GATE_NORMALIZER = 16.0
LOG2_E = math.log2(math.e)

V7X_VMEM_BYTES = 64 * 1024 * 1024
VMEM_HEADROOM_BYTES = 6 * 1024 * 1024
LANES = 128
SUBLANES = 8
SMALL_LEVELS = tuple(2 ** i for i in range(int(math.log2(SUBLANES))))
MLP_DOWN_COLS = 512
MLP_ROW_SPLIT = 2

def _vmem_limit(declared_bytes):
    return int(min(V7X_VMEM_BYTES - 2 * 1024 * 1024, declared_bytes + VMEM_HEADROOM_BYTES))


def _nbytes(shape, dtype):
    return int(np.prod(shape)) * jnp.dtype(dtype).itemsize


def _dot(a, b):
    return jnp.dot(a, b, preferred_element_type=F32)


def _dot_nt(a, b):
    return lax.dot_general(a, b, (((1,), (1,)), ((), ())), preferred_element_type=F32)


def _dot_tn(a, b):
    return lax.dot_general(a, b, (((0,), (0,)), ((), ())), preferred_element_type=F32)


def _sigmoid(x):
    return 1.0 / (1.0 + jnp.exp(-x))


def _silu(x):
    return x * _sigmoid(x)


def _rms_scale(x, w):
    ms = jnp.mean(x * x, axis=-1, keepdims=True)
    return x * lax.rsqrt(ms + NORM_EPS) * w


def _split3_bf16(x):
    a = x.astype(BF16)
    r = x - a.astype(F32)
    b = r.astype(BF16)
    return a, b, (r - b.astype(F32)).astype(BF16)


def _project_row_tile(x_ref, nw_ref, xn_ref, project, row_split):
    first = pl.program_id(1) == 0
    tm = x_ref.shape[0]

    @pl.when(first)
    def _():
        for r in range(0, tm, tm // row_split):
            rows = slice(r, r + tm // row_split)
            xn = _rms_scale(x_ref[rows, :], nw_ref[...]).astype(BF16)
            xn_ref[rows, :] = xn
            project(xn, rows, True)

    @pl.when(jnp.logical_not(first))
    def _():
        project(xn_ref[...], slice(None), False)


def _hgrn_inproj_kernel(x_ref, nw_ref, wq_ref, wf_ref, wi_ref, wg_ref, lbl_ref, gain_ref,
                        q_ref, k_ref, g1_ref, g2_ref, g3_ref, v_ref, gt_ref, xn_ref, *, layer, row_split):
    def project(xn, rows, first_step):
        lg = lbl_ref[...]
        e = jnp.exp(lg - jnp.max(lg, axis=0, keepdims=True))
        lb = jnp.sum(e[:layer + 1], axis=0, keepdims=True) / jnp.sum(e, axis=0, keepdims=True)
        forget = lb + (1.0 - lb) * _sigmoid(_dot(xn, wf_ref[...]))
        k_ref[rows, :] = 1.0 - forget
        g1_ref[rows, :], g2_ref[rows, :], g3_ref[rows, :] = _split3_bf16(jnp.log2(forget))
        gt_ref[rows, :] = _silu(_dot(xn, wg_ref[...])) * gain_ref[...]
        q_ref[rows, :] = _silu(_dot(xn, wq_ref[...]))
        v_ref[rows, :] = _dot(xn, wi_ref[...]).astype(BF16)

    _project_row_tile(x_ref, nw_ref, xn_ref, project, row_split)


def _hgrn_inproj(h, norm_w, w_in, lb_logits, gain_row, layer, *, tm=1024, tn=512):
    m, d = h.shape
    df = w_in.shape[1] // 4
    tiles = df // tn
    kern = functools.partial(_hgrn_inproj_kernel, layer=layer, row_split=2)
    bf_out = jax.ShapeDtypeStruct((m, df), BF16)
    f32_out = jax.ShapeDtypeStruct((m, df), F32)
    declared = (2 * _nbytes((tm, d), F32) + _nbytes((tm, d), BF16) + 8 * _nbytes((d, tn), BF16)
                + 2 * (4 * _nbytes((tm, tn), BF16) + 3 * _nbytes((tm, tn), F32)))
    w_spec =[pl.BlockSpec((d, tn), functools.partial(lambda i, j, g: (0, g * tiles + j), g=g))
              for g in range(4)]
    out_spec = pl.BlockSpec((tm, tn), lambda i, j: (i, j))
    return pl.pallas_call(
        kern,
        grid=(m // tm, tiles),
        in_specs=[
            pl.BlockSpec((tm, d), lambda i, j: (i, 0)),
            pl.BlockSpec((1, d), lambda i, j: (0, 0)),
            *w_spec,
            pl.BlockSpec((lb_logits.shape[0], tn), lambda i, j: (0, j)),
            pl.BlockSpec((1, tn), lambda i, j: (0, j)),
        ],
        out_specs=[out_spec] * 7,
        out_shape=[f32_out] * 2 + [bf_out] * 4 + [f32_out],
        scratch_shapes=[pltpu.VMEM((tm, d), BF16)],
        compiler_params=pltpu.CompilerParams(
            dimension_semantics=("arbitrary", "arbitrary"),
            vmem_limit_bytes=_vmem_limit(declared)),
        name="hgrn_inproj",
    )(h, norm_w.reshape(1, d), w_in, w_in, w_in, w_in, lb_logits, gain_row)


def _gla_inproj_kernel(x_ref, nw_ref, wq_ref, wk_ref, wv_ref, wg_ref, wr_ref, wgk_ref, bgk_ref,
                       gain_ref, q_ref, k_ref, g1_ref, g2_ref, g3_ref, v_ref, gt_ref,
                       xn_ref, r_ref, *, q_scale, row_split):
    def project(xn, rows, first_step):
        if first_step:
            r_ref[rows, :] = _dot(xn, wr_ref[...])
        r = r_ref[rows, :]
        w = wgk_ref[...]
        r_hi = r.astype(BF16)
        r_lo = (r - r_hi.astype(F32)).astype(BF16)
        w_hi = w.astype(BF16)
        w_lo = (w - w_hi.astype(F32)).astype(BF16)
        z = _dot(r_hi, w_hi) + (_dot(r_hi, w_lo) + _dot(r_lo, w_hi)) + bgk_ref[...]
        log_sig = jnp.minimum(z, 0.0) - jnp.log(1.0 + jnp.exp(-jnp.abs(z)))
        g1_ref[rows, :], g2_ref[rows, :], g3_ref[rows, :] = _split3_bf16(
            log_sig * (LOG2_E / GLA_GATE_NORMALIZER))
        gt_ref[rows, :] = _silu(_dot_nt(xn, wg_ref[...])) * gain_ref[...]
        q_ref[rows, :] = _dot_nt(xn, wq_ref[...]) * q_scale
        k_ref[rows, :] = _dot_nt(xn, wk_ref[...])
        v_ref[rows, :] = _dot_nt(xn, wv_ref[...]).astype(BF16)

    _project_row_tile(x_ref, nw_ref, xn_ref, project, row_split)


def _gla_inproj(h, norm_w, w_in_t, w_r, w_gk, b_gk, gain_row, *, key_dim, value_dim, heads,
                tm=1024, steps=4):
    m, d = h.shape
    tk, tv = key_dim // steps, value_dim // steps
    kern = functools.partial(_gla_inproj_kernel, q_scale=float((key_dim // heads) ** -0.5),
                             row_split=1)
    declared = (2 * _nbytes((tm, d), F32) + _nbytes((tm, d), BF16)
                + 4 * _nbytes((d, tk), BF16) + 4 * _nbytes((d, tv), BF16)
                + 2 * _nbytes((d, LANES), BF16) + _nbytes((tm, LANES), F32)
                + 2 * (3 * _nbytes((tm, tk), BF16) + 2 * _nbytes((tm, tk), F32)
                       + _nbytes((tm, tv), F32) + _nbytes((tm, tv), BF16)))
    k_first, v_first, g_first = key_dim // tk, 2 * key_dim // tv, (2 * key_dim + value_dim) // tv
    key_out = pl.BlockSpec((tm, tk), lambda i, j: (i, j))
    val_out = pl.BlockSpec((tm, tv), lambda i, j: (i, j))
    key_bf = jax.ShapeDtypeStruct((m, key_dim), BF16)
    key_f32 = jax.ShapeDtypeStruct((m, key_dim), F32)
    return pl.pallas_call(
        kern,
        grid=(m // tm, steps),
        in_specs=[
            pl.BlockSpec((tm, d), lambda i, j: (i, 0)),
            pl.BlockSpec((1, d), lambda i, j: (0, 0)),
            pl.BlockSpec((tk, d), lambda i, j: (j, 0)),
            pl.BlockSpec((tk, d), lambda i, j: (k_first + j, 0)),
            pl.BlockSpec((tv, d), lambda i, j: (v_first + j, 0)),
            pl.BlockSpec((tv, d), lambda i, j: (g_first + j, 0)),
            pl.BlockSpec((d, LANES), lambda i, j: (0, 0)),
            pl.BlockSpec((LANES, tk), lambda i, j: (0, j)),
            pl.BlockSpec((1, tk), lambda i, j: (0, j)),
            pl.BlockSpec((1, tv), lambda i, j: (0, j)),
        ],
        out_specs=[key_out] * 5 + [val_out, val_out],
        out_shape=[key_f32] * 2 + [key_bf] * 3 + [jax.ShapeDtypeStruct((m, value_dim), BF16),
                                     jax.ShapeDtypeStruct((m, value_dim), F32)],
        scratch_shapes=[pltpu.VMEM((tm, d), BF16), pltpu.VMEM((tm, LANES), F32)],
        compiler_params=pltpu.CompilerParams(
            dimension_semantics=("arbitrary", "arbitrary"),
            vmem_limit_bytes=_vmem_limit(declared)),
        name="gla_inproj",
    )(h, norm_w.reshape(1, d), w_in_t, w_in_t, w_in_t, w_in_t, w_r, w_gk, b_gk.reshape(1, key_dim), gain_row)


def _pair_levels(c):
    i = np.arange(c)[:, None]
    j = np.arange(c)[None, :]
    x = i ^ j
    lvl = np.where(x > 0, np.floor(np.log2(np.maximum(x, 1))).astype(np.int64) + 1, 0)
    return np.where(j > i, -1, lvl).astype(np.int32)


def _row_signs(c, dk):
    r = np.arange(c)[None, :, None]
    s = np.array(SMALL_LEVELS)[:, None, None]
    return np.broadcast_to(np.where(r & s, 1.0, -1.0), (len(SMALL_LEVELS), c, dk)).astype(np.float32)


def _reference_rows(b, b_ref, row, s, c):
    dk = b.shape[-1]
    if s == 1:
        return jnp.where((row & 1) != 0, pltpu.roll(b, 1, 0), b)
    if s == 2:
        lo = [jnp.broadcast_to(b_ref[pl.ds(8 * v + 1, 1), :], (8, dk)) for v in range(c // 8)]
        hi = [jnp.broadcast_to(b_ref[pl.ds(8 * v + 5, 1), :], (8, dk)) for v in range(c // 8)]
        return jnp.where((row & 4) != 0, jnp.concatenate(hi, axis=0), jnp.concatenate(lo, axis=0))
    parts = [jnp.broadcast_to(b_ref[pl.ds(m * 2 * s + s - 1, 1), :], (2 * s, dk))
             for m in range(c // (2 * s))]
    return parts[0] if len(parts) == 1 else jnp.concatenate(parts, axis=0)


def _level_exponent(b, b_ref, sgn_ref, row, s, c):
    dk = b.shape[-1]
    if s in SMALL_LEVELS:
        return (b - _reference_rows(b, b_ref, row, s, c)) * sgn_ref[SMALL_LEVELS.index(s)]
    parts = []
    for m in range(c // (2 * s)):
        lo = m * 2 * s
        r = jnp.broadcast_to(b_ref[pl.ds(lo + s - 1, 1), :], (s, dk))
        parts += [r - b[lo:lo + s], b[lo + s:lo + 2 * s] - r]
    return jnp.concatenate(parts, axis=0)


def _gla_chunk_kernel(*refs, chunk, n_chunks, group, n_cast):
    lvl_ref, sgn_ref, q_ref, k_ref, g1_ref, g2_ref, g3_ref, v_ref, gt_ref = refs[:9]
    cast_src = refs[9:9 + n_cast]
    o_ref = refs[9 + n_cast]
    cast_dst = refs[10 + n_cast:10 + 2 * n_cast]
    st_ref, b0_ref, b1_ref, op_ref = refs[10 + 2 * n_cast:]
    c = chunk

    for src, dst in zip(cast_src, cast_dst):
        dst[...] = src[...].astype(dst.dtype)
    dk = q_ref.shape[-1] // group
    dv = v_ref.shape[-1] // group

    @pl.when(pl.program_id(2) == 0)
    def _():
        st_ref[...] = jnp.zeros_like(st_ref)
        op_ref[...] = jnp.zeros_like(op_ref)

    lvl = lvl_ref[...]
    tril = jnp.where(lvl >= 0, 1.0, 0.0).astype(BF16)
    row = lax.broadcasted_iota(jnp.int32, (c, dk), 0)
    heads = range(group)
    kcols = [slice(hh * dk, (hh + 1) * dk) for hh in heads]
    vcols = [slice(hh * dv, (hh + 1) * dv) for hh in heads]

    def chunk_rows(ci):
        return pl.ds(pl.multiple_of(ci * c, c), c)

    def cumulative_decay(rows, b_ref):
        for hh in heads:
            b_ref[hh] = (_dot(tril, g1_ref[rows, kcols[hh]])
                         + (_dot(tril, g2_ref[rows, kcols[hh]]) + _dot(tril, g3_ref[rows, kcols[hh]])))

    def finish(rows):
        for hh in heads:
            o = op_ref[hh]
            ms = jnp.mean(o * o, axis=-1, keepdims=True)
            on = o * lax.rsqrt(ms + NORM_EPS) * gt_ref[rows, vcols[hh]]
            o_ref[rows, vcols[hh]] = on.astype(o_ref.dtype)

    def chunk_step(ci, b_cur_ref, b_next_ref):
        rows = chunk_rows(ci)
        cumulative_decay(chunk_rows(jnp.minimum(ci + 1, n_chunks - 1)), b_next_ref)
        finish(chunk_rows(jnp.maximum(ci - 1, 0)))

        for hh in heads:
            q = q_ref[rows, kcols[hh]]
            k = k_ref[rows, kcols[hh]]
            b = b_cur_ref[hh]
            b_last = b_cur_ref[hh, pl.ds(c - 1, 1), :]
            st = st_ref[hh]
            op_ref[hh] = _dot_nt((q * jnp.exp2(b)).astype(BF16), st.astype(BF16))
            kl = (k * jnp.exp2(b_last - b)).astype(BF16)
            st_ref[hh] = st * jnp.exp2(b_last) + _dot_tn(v_ref[rows, vcols[hh]], kl)

        scores = []
        for hh in heads:
            q = q_ref[rows, kcols[hh]].astype(BF16)
            k = k_ref[rows, kcols[hh]].astype(BF16)
            b = b_cur_ref[hh]
            sc = jnp.where(lvl == 0, _dot_nt(q, k), 0.0)
            s = 1
            level = 1
            while s < c:
                e = jnp.exp2(_level_exponent(b, b_cur_ref.at[hh], sgn_ref, row, s, c)).astype(BF16)
                p = _dot_nt(q * e, k * e)
                sc = jnp.where(lvl == level, p, sc)
                s *= 2
                level += 1
            scores.append(sc.astype(BF16))
        for hh in heads:
            op_ref[hh] += _dot(scores[hh], v_ref[rows, vcols[hh]])

    def body(pair, carry):
        chunk_step(2 * pair, b0_ref, b1_ref)
        chunk_step(2 * pair + 1, b1_ref, b0_ref)
        return carry

    cumulative_decay(chunk_rows(0), b0_ref)
    lax.fori_loop(0, n_chunks // 2, body, 0)
    finish(chunk_rows(n_chunks - 1))


def _gla_chunk(q, k, g_split, v, gate, *, batch, heads, group, rows_per_step, chunk=CHUNK,
               cast_jobs=()):
    m, kd = q.shape
    vd = v.shape[1]
    dk, dv = kd // heads, vd // heads
    gk, gv = group * dk, group * dv
    seq = m // batch
    tc = min(rows_per_step, seq)
    steps = seq // tc
    assert tc % (2 * chunk) == 0
    kern = functools.partial(_gla_chunk_kernel, chunk=chunk, n_chunks=tc // chunk, group=group,
                             n_cast=len(cast_jobs))
    lvl = jnp.asarray(_pair_levels(chunk))
    sgn = jnp.asarray(_row_signs(chunk, dk))

    def rows_map(b, h, t):
        return b * steps + t, h

    declared = (2 * (3 * _nbytes((tc, gk), BF16) + 2 * _nbytes((tc, gk), F32)
                     + 2 * _nbytes((tc, gv), BF16) + _nbytes((tc, gv), F32))
                + _nbytes((group, dv, dk), F32) + 2 * _nbytes((group, chunk, dk), F32)
                + _nbytes((group, chunk, dv), F32))
    key_spec = pl.BlockSpec((tc, gk), rows_map)
    val_spec = pl.BlockSpec((tc, gv), rows_map)

    grid = (batch, heads // group, steps)
    n_steps = batch * (heads // group) * steps

    def flat_step(b, h, t):
        return (b * (heads // group) + h) * steps + t

    cast_in_specs, cast_out_specs, cast_out_shapes = [], [], []
    for w, layer, rows in cast_jobs:
        cols = w.shape[2]
        rb = rows // n_steps
        assert rb * n_steps == rows and rb % 16 == 0, (w.shape, n_steps)
        cast_in_specs.append(pl.BlockSpec(
            (None, rb, cols), functools.partial(lambda b, h, t, layer: (layer, flat_step(b, h, t), 0),
                                                layer=layer)))
        cast_out_specs.append(pl.BlockSpec((rb, cols), lambda b, h, t: (flat_step(b, h, t), 0)))
        cast_out_shapes.append(jax.ShapeDtypeStruct((rows, cols), BF16))
        declared += 2 * (_nbytes((rb, cols), F32) + _nbytes((rb, cols), BF16))

    outs = pl.pallas_call(
        kern,
        grid=grid,
        in_specs=[pl.BlockSpec(lvl.shape, lambda b, h, t: (0, 0)),
                  pl.BlockSpec(sgn.shape, lambda b, h, t: (0, 0, 0))] + [key_spec] * 5 + [val_spec] * 2
                 + cast_in_specs,
        out_specs=[val_spec] + cast_out_specs,
        out_shape=[jax.ShapeDtypeStruct((m, vd), BF16)] + cast_out_shapes,
        scratch_shapes=[pltpu.VMEM((group, dv, dk), F32), pltpu.VMEM((group, chunk, dk), F32),
                        pltpu.VMEM((group, chunk, dk), F32), pltpu.VMEM((group, chunk, dv), F32)],
        compiler_params=pltpu.CompilerParams(
            dimension_semantics=("arbitrary", "arbitrary", "arbitrary"),
            vmem_limit_bytes=_vmem_limit(declared)),
        name="gla_chunk",
    )(lvl, sgn, q, k, *g_split, v, gate, *[job[0] for job in cast_jobs])
    return outs[0], outs[1:]


def _outproj_kernel(o_ref, w_ref, h_ref, nw_ref, out_ref):
    mixed = _dot(o_ref[...], w_ref[...])
    out_ref[...] = h_ref[...] + _rms_scale(mixed, nw_ref[...])


def _outproj(o, w_out, h, norm_w, *, tm=512):
    m, d = h.shape
    dv = o.shape[1]
    declared = (2 * _nbytes((tm, dv), BF16) + 2 * _nbytes((dv, d), BF16) + 4 * _nbytes((tm, d), F32))
    return pl.pallas_call(
        _outproj_kernel,
        grid=(m // tm,),
        in_specs=[
            pl.BlockSpec((tm, dv), lambda i: (i, 0)),
            pl.BlockSpec((dv, d), lambda i: (0, 0)),
            pl.BlockSpec((tm, d), lambda i: (i, 0)),
            pl.BlockSpec((1, d), lambda i: (0, 0)),
        ],
        out_specs=pl.BlockSpec((tm, d), lambda i: (i, 0)),
        out_shape=jax.ShapeDtypeStruct((m, d), F32),
        compiler_params=pltpu.CompilerParams(
            dimension_semantics=("arbitrary",),
            vmem_limit_bytes=_vmem_limit(declared)),
        name="outproj",
    )(o, w_out, h, norm_w.reshape(1, d))


def _mlp_kernel(h_ref, pre_ref, wu_ref, wd_ref, post_ref, out_ref, xn_ref):
    kstep = pl.program_id(1)
    last = pl.num_programs(1) - 1
    tm = out_ref.shape[0]
    halves = [slice(r, r + tm // MLP_ROW_SPLIT) for r in range(0, tm, tm // MLP_ROW_SPLIT)]

    def hidden(xn):
        u = jnp.maximum(_dot(xn, wu_ref[...]), 0.0)
        return (u * u).astype(BF16)

    def down(u2, rows, add_to):
        for n in range(0, out_ref.shape[1], MLP_DOWN_COLS):
            cols = slice(n, n + MLP_DOWN_COLS)
            part = _dot(u2, wd_ref[:, cols])
            out_ref[rows, cols] = part if add_to is None else add_to[rows, cols] + part

    @pl.when(kstep == 0)
    def _():
        for rows in halves:
            xn = _rms_scale(h_ref[rows, :], pre_ref[...]).astype(BF16)
            xn_ref[rows, :] = xn
            down(hidden(xn), rows, None)

    @pl.when((kstep > 0) & (kstep < last))
    def _():
        down(hidden(xn_ref[...]), slice(None), out_ref)

    @pl.when(kstep == last)
    def _():
        for rows in halves:
            down(hidden(xn_ref[rows, :]), rows, out_ref)
            out_ref[rows, :] = h_ref[rows, :] + _rms_scale(out_ref[rows, :], post_ref[...])


def _mlp(h, pre_w, w_up, w_down, post_w, *, tm=1024, tk=1024):
    m, d = h.shape
    dff = w_up.shape[1]
    declared = (4 * _nbytes((tm, d), F32) + 2 * _nbytes((d, tk), BF16) + 2 * _nbytes((tk, d), BF16)
                + _nbytes((tm, d), BF16) + 2 * _nbytes((tm, tk), F32))
    return pl.pallas_call(
        _mlp_kernel,
        grid=(m // tm, dff // tk),
        in_specs=[
            pl.BlockSpec((tm, d), lambda i, k: (i, 0)),
            pl.BlockSpec((1, d), lambda i, k: (0, 0)),
            pl.BlockSpec((d, tk), lambda i, k: (0, k)),
            pl.BlockSpec((tk, d), lambda i, k: (k, 0)),
            pl.BlockSpec((1, d), lambda i, k: (0, 0)),
        ],
        out_specs=pl.BlockSpec((tm, d), lambda i, k: (i, 0)),
        out_shape=jax.ShapeDtypeStruct((m, d), F32),
        scratch_shapes=[pltpu.VMEM((tm, d), BF16)],
        compiler_params=pltpu.CompilerParams(
            dimension_semantics=("arbitrary", "arbitrary"),
            vmem_limit_bytes=_vmem_limit(declared)),
        name="mlp",
    )(h, pre_w.reshape(1, d), w_up, w_down, post_w.reshape(1, d))


def kernel(x, norm_mix_pre, norm_mix_post, norm_mlp_pre, norm_mlp_post, hgrn_w_in, hgrn_lb_logits, hgrn_norm, hgrn_w_out, gla_w_in, gla_w_gk, gla_b_gk, gla_norm, gla_w_out, mlp_w_up, mlp_w_down):
    batch, seq, d = x.shape
    depth = norm_mix_pre.shape[0]
    h = x.reshape(batch * seq, d)
    later = {}
    for layer in range(depth):
        j = layer // 2
        if layer % 2 == 0:
            if layer > 0:
                later["hgrn_in", j] = hgrn_w_in
            later["hgrn_out", j] = hgrn_w_out
        else:
            later["gla_in", j] = jnp.swapaxes(gla_w_in, 1, 2)
            later["gla_out", j] = gla_w_out
        later["mlp_up", layer] = mlp_w_up
        later["mlp_down", layer] = mlp_w_down
    bf16_w = {("hgrn_in", 0): hgrn_w_in[0].astype(BF16)}
    gla_rank = gla_w_gk.shape[1]
    gla_rows = gla_w_in.shape[2] - gla_rank

    for layer in range(depth):
        j = layer // 2
        cast_jobs = [(w, key[1], gla_rows if key[0] == "gla_in" else w.shape[1])
                     for key, w in later.items()] if layer == 0 else []
        if layer % 2 == 0:
            gain_row = jnp.tile(hgrn_norm[j], HGRN_HEADS).reshape(1, -1)
            q, k, g1, g2, g3, v, gate = _hgrn_inproj(h, norm_mix_pre[layer], bf16_w["hgrn_in", j],
                                                     hgrn_lb_logits, gain_row, layer)
            o, cast = _gla_chunk(q, k, (g1, g2, g3), v, gate, batch=batch, heads=HGRN_HEADS,
                                 group=16, rows_per_step=256, cast_jobs=cast_jobs)
            w_out = ("hgrn_out", j)
        else:
            key_dim = gla_w_gk.shape[2]
            rank = gla_w_gk.shape[1]
            value_dim = (gla_w_in.shape[2] - 2 * key_dim - rank) // 2
            n_main = 2 * key_dim + 2 * value_dim
            w_r = jnp.pad(gla_w_in[j, :, n_main:], ((0, 0), (0, LANES - rank))).astype(BF16)
            w_gk = jnp.pad(gla_w_gk[j], ((0, LANES - rank), (0, 0)))
            gain_row = jnp.tile(gla_norm[j], GLA_HEADS).reshape(1, -1)
            q, k, g1, g2, g3, v, gate = _gla_inproj(h, norm_mix_pre[layer], bf16_w["gla_in", j], w_r,
                                                    w_gk, gla_b_gk[j], gain_row,
                                                    key_dim=key_dim, value_dim=value_dim, heads=GLA_HEADS)
            o, cast = _gla_chunk(q, k, (g1, g2, g3), v, gate, batch=batch, heads=GLA_HEADS,
                                 group=4, rows_per_step=512, cast_jobs=cast_jobs)
            w_out = ("gla_out", j)
        if layer == 0:
            bf16_w.update(zip(later.keys(), cast))
        h = _outproj(o, bf16_w[w_out], h, norm_mix_post[layer])
        h = _mlp(h, norm_mlp_pre[layer], bf16_w["mlp_up", layer], bf16_w["mlp_down", layer],
                 norm_mlp_post[layer])
    return h.reshape(batch, seq, d)
```

```python
import functools
import math

import numpy as np
import jax
import jax.numpy as jnp
from jax import lax
from jax.experimental import pallas as pl
from jax.experimental.pallas import tpu as pltpu

F32 = jnp.float32
BF16 = jnp.bfloat16

NORM_EPS = 1e-6
CHUNK = 64
HGRN_HEADS = 16
GLA_HEADS = 4
GLA_GATE_NORMALIZER = 16.0
LOG2_E = math.log2(math.e)

V7X_VMEM_BYTES = 64 * 1024 * 1024
VMEM_HEADROOM_BYTES = 6 * 1024 * 1024
LANES = 128
SUBLANES = 8
SMALL_LEVELS = tuple(2 ** i for i in range(int(math.log2(SUBLANES))))
MLP_DOWN_COLS = 512
MLP_ROW_SPLIT = 2

def _vmem_limit(declared_bytes):
    return int(min(V7X_VMEM_BYTES - 2 * 1024 * 1024, declared_bytes + VMEM_HEADROOM_BYTES))


def _nbytes(shape, dtype):
    return int(np.prod(shape)) * jnp.dtype(dtype).itemsize


def _dot(a, b):
    return jnp.dot(a, b, preferred_element_type=F32)


def _dot_nt(a, b):
    return lax.dot_general(a, b, (((1,), (1,)), ((), ())), preferred_element_type=F32)


def _dot_tn(a, b):
    return lax.dot_general(a, b, (((0,), (0,)), ((), ())), preferred_element_type=F32)


def _sigmoid(x):
    return 1.0 / (1.0 + jnp.exp(-x))


def _silu(x):
    return x * _sigmoid(x)


def _rms_scale(x, w):
    ms = jnp.mean(x * x, axis=-1, keepdims=True)
    return x * lax.rsqrt(ms + NORM_EPS) * w


def _split3_bf16(x):
    a = x.astype(BF16)
    r = x - a.astype(F32)
    b = r.astype(BF16)
    return a, b, (r - b.astype(F32)).astype(BF16)


def _project_row_tile(x_ref, nw_ref, xn_ref, project, row_split):
    first = pl.program_id(1) == 0
    tm = x_ref.shape[0]

    @pl.when(first)
    def _():
        for r in range(0, tm, tm // row_split):
            rows = slice(r, r + tm // row_split)
            xn = _rms_scale(x_ref[rows, :], nw_ref[...]).astype(BF16)
            xn_ref[rows, :] = xn
            project(xn, rows, True)

    @pl.when(jnp.logical_not(first))
    def _():
        project(xn_ref[...], slice(None), False)


def _hgrn_inproj_kernel(x_ref, nw_ref, wq_ref, wf_ref, wi_ref, wg_ref, lbl_ref, gain_ref,
                        q_ref, k_ref, g1_ref, g2_ref, g3_ref, v_ref, gt_ref, xn_ref, *, layer, row_split):
    def project(xn, rows, first_step):
        lg = lbl_ref[...]
        e = jnp.exp(lg - jnp.max(lg, axis=0, keepdims=True))
        lb = jnp.sum(e[:layer + 1], axis=0, keepdims=True) / jnp.sum(e, axis=0, keepdims=True)
        forget = lb + (1.0 - lb) * _sigmoid(_dot(xn, wf_ref[...]))
        k_ref[rows, :] = 1.0 - forget
        g1_ref[rows, :], g2_ref[rows, :], g3_ref[rows, :] = _split3_bf16(jnp.log2(forget))
        gt_ref[rows, :] = _silu(_dot(xn, wg_ref[...])) * gain_ref[...]
        q_ref[rows, :] = _silu(_dot(xn, wq_ref[...]))
        v_ref[rows, :] = _dot(xn, wi_ref[...]).astype(BF16)

    _project_row_tile(x_ref, nw_ref, xn_ref, project, row_split)


def _hgrn_inproj(h, norm_w, w_in, lb_logits, gain_row, layer, *, tm=1024, tn=512):
    m, d = h.shape
    df = w_in.shape[1] // 4
    tiles = df // tn
    kern = functools.partial(_hgrn_inproj_kernel, layer=layer, row_split=2)
    bf_out = jax.ShapeDtypeStruct((m, df), BF16)
    f32_out = jax.ShapeDtypeStruct((m, df), F32)
    declared = (2 * _nbytes((tm, d), F32) + _nbytes((tm, d), BF16) + 8 * _nbytes((d, tn), BF16)
                + 2 * (4 * _nbytes((tm, tn), BF16) + 3 * _nbytes((tm, tn), F32)))
    w_spec =[pl.BlockSpec((d, tn), functools.partial(lambda i, j, g: (0, g * tiles + j), g=g))
              for g in range(4)]
    out_spec = pl.BlockSpec((tm, tn), lambda i, j: (i, j))
    return pl.pallas_call(
        kern,
        grid=(m // tm, tiles),
        in_specs=[
            pl.BlockSpec((tm, d), lambda i, j: (i, 0)),
            pl.BlockSpec((1, d), lambda i, j: (0, 0)),
            *w_spec,
            pl.BlockSpec((lb_logits.shape[0], tn), lambda i, j: (0, j)),
            pl.BlockSpec((1, tn), lambda i, j: (0, j)),
        ],
        out_specs=[out_spec] * 7,
        out_shape=[f32_out] * 2 + [bf_out] * 4 + [f32_out],
        scratch_shapes=[pltpu.VMEM((tm, d), BF16)],
        compiler_params=pltpu.CompilerParams(
            dimension_semantics=("arbitrary", "arbitrary"),
            vmem_limit_bytes=_vmem_limit(declared)),
        name="hgrn_inproj",
    )(h, norm_w.reshape(1, d), w_in, w_in, w_in, w_in, lb_logits, gain_row)


def _gla_inproj_kernel(x_ref, nw_ref, wq_ref, wk_ref, wv_ref, wg_ref, wr_ref, wgk_ref, bgk_ref,
                       gain_ref, q_ref, k_ref, g1_ref, g2_ref, g3_ref, v_ref, gt_ref,
                       xn_ref, r_ref, *, q_scale, row_split):
    def project(xn, rows, first_step):
        if first_step:
            r_ref[rows, :] = _dot(xn, wr_ref[...])
        r = r_ref[rows, :]
        w = wgk_ref[...]
        r_hi = r.astype(BF16)
        r_lo = (r - r_hi.astype(F32)).astype(BF16)
        w_hi = w.astype(BF16)
        w_lo = (w - w_hi.astype(F32)).astype(BF16)
        z = _dot(r_hi, w_hi) + (_dot(r_hi, w_lo) + _dot(r_lo, w_hi)) + bgk_ref[...]
        log_sig = jnp.minimum(z, 0.0) - jnp.log(1.0 + jnp.exp(-jnp.abs(z)))
        g1_ref[rows, :], g2_ref[rows, :], g3_ref[rows, :] = _split3_bf16(
            log_sig * (LOG2_E / GLA_GATE_NORMALIZER))
        gt_ref[rows, :] = _silu(_dot_nt(xn, wg_ref[...])) * gain_ref[...]
        q_ref[rows, :] = _dot_nt(xn, wq_ref[...]) * q_scale
        k_ref[rows, :] = _dot_nt(xn, wk_ref[...])
        v_ref[rows, :] = _dot_nt(xn, wv_ref[...]).astype(BF16)

    _project_row_tile(x_ref, nw_ref, xn_ref, project, row_split)


def _gla_inproj(h, norm_w, w_in_t, w_r, w_gk, b_gk, gain_row, *, key_dim, value_dim, heads,
                tm=1024, steps=4):
    m, d = h.shape
    tk, tv = key_dim // steps, value_dim // steps
    kern = functools.partial(_gla_inproj_kernel, q_scale=float((key_dim // heads) ** -0.5),
                             row_split=1)
    declared = (2 * _nbytes((tm, d), F32) + _nbytes((tm, d), BF16)
                + 4 * _nbytes((d, tk), BF16) + 4 * _nbytes((d, tv), BF16)
                + 2 * _nbytes((d, LANES), BF16) + _nbytes((tm, LANES), F32)
                + 2 * (3 * _nbytes((tm, tk), BF16) + 2 * _nbytes((tm, tk), F32)
                       + _nbytes((tm, tv), F32) + _nbytes((tm, tv), BF16)))
    k_first, v_first, g_first = key_dim // tk, 2 * key_dim // tv, (2 * key_dim + value_dim) // tv
    key_out = pl.BlockSpec((tm, tk), lambda i, j: (i, j))
    val_out = pl.BlockSpec((tm, tv), lambda i, j: (i, j))
    key_bf = jax.ShapeDtypeStruct((m, key_dim), BF16)
    key_f32 = jax.ShapeDtypeStruct((m, key_dim), F32)
    return pl.pallas_call(
        kern,
        grid=(m // tm, steps),
        in_specs=[
            pl.BlockSpec((tm, d), lambda i, j: (i, 0)),
            pl.BlockSpec((1, d), lambda i, j: (0, 0)),
            pl.BlockSpec((tk, d), lambda i, j: (j, 0)),
            pl.BlockSpec((tk, d), lambda i, j: (k_first + j, 0)),
            pl.BlockSpec((tv, d), lambda i, j: (v_first + j, 0)),
            pl.BlockSpec((tv, d), lambda i, j: (g_first + j, 0)),
            pl.BlockSpec((d, LANES), lambda i, j: (0, 0)),
            pl.BlockSpec((LANES, tk), lambda i, j: (0, j)),
            pl.BlockSpec((1, tk), lambda i, j: (0, j)),
            pl.BlockSpec((1, tv), lambda i, j: (0, j)),
        ],
        out_specs=[key_out] * 5 + [val_out, val_out],
        out_shape=[key_f32] * 2 + [key_bf] * 3 + [jax.ShapeDtypeStruct((m, value_dim), BF16),
                                     jax.ShapeDtypeStruct((m, value_dim), F32)],
        scratch_shapes=[pltpu.VMEM((tm, d), BF16), pltpu.VMEM((tm, LANES), F32)],
        compiler_params=pltpu.CompilerParams(
            dimension_semantics=("arbitrary", "arbitrary"),
            vmem_limit_bytes=_vmem_limit(declared)),
        name="gla_inproj",
    )(h, norm_w.reshape(1, d), w_in_t, w_in_t, w_in_t, w_in_t, w_r, w_gk, b_gk.reshape(1, key_dim), gain_row)


def _pair_levels(c):
    i = np.arange(c)[:, None]
    j = np.arange(c)[None, :]
    x = i ^ j
    lvl = np.where(x > 0, np.floor(np.log2(np.maximum(x, 1))).astype(np.int64) + 1, 0)
    return np.where(j > i, -1, lvl).astype(np.int32)


def _row_signs(c, dk):
    r = np.arange(c)[None, :, None]
    s = np.array(SMALL_LEVELS)[:, None, None]
    return np.broadcast_to(np.where(r & s, 1.0, -1.0), (len(SMALL_LEVELS), c, dk)).astype(np.float32)


def _reference_rows(b, b_ref, row, s, c):
    dk = b.shape[-1]
    if s == 1:
        return jnp.where((row & 1) != 0, pltpu.roll(b, 1, 0), b)
    if s == 2:
        lo = [jnp.broadcast_to(b_ref[pl.ds(8 * v + 1, 1), :], (8, dk)) for v in range(c // 8)]
        hi = [jnp.broadcast_to(b_ref[pl.ds(8 * v + 5, 1), :], (8, dk)) for v in range(c // 8)]
        return jnp.where((row & 4) != 0, jnp.concatenate(hi, axis=0), jnp.concatenate(lo, axis=0))
    parts = [jnp.broadcast_to(b_ref[pl.ds(m * 2 * s + s - 1, 1), :], (2 * s, dk))
             for m in range(c // (2 * s))]
    return parts[0] if len(parts) == 1 else jnp.concatenate(parts, axis=0)


def _level_exponent(b, b_ref, sgn_ref, row, s, c):
    dk = b.shape[-1]
    if s in SMALL_LEVELS:
        return (b - _reference_rows(b, b_ref, row, s, c)) * sgn_ref[SMALL_LEVELS.index(s)]
    parts = []
    for m in range(c // (2 * s)):
        lo = m * 2 * s
        r = jnp.broadcast_to(b_ref[pl.ds(lo + s - 1, 1), :], (s, dk))
        parts += [r - b[lo:lo + s], b[lo + s:lo + 2 * s] - r]
    return jnp.concatenate(parts, axis=0)


def _gla_chunk_kernel(*refs, chunk, n_chunks, group, n_cast):
    lvl_ref, sgn_ref, q_ref, k_ref, g1_ref, g2_ref, g3_ref, v_ref, gt_ref = refs[:9]
    cast_src = refs[9:9 + n_cast]
    o_ref = refs[9 + n_cast]
    cast_dst = refs[10 + n_cast:10 + 2 * n_cast]
    st_ref, b0_ref, b1_ref, op_ref = refs[10 + 2 * n_cast:]
    c = chunk

    for src, dst in zip(cast_src, cast_dst):
        dst[...] = src[...].astype(dst.dtype)
    dk = q_ref.shape[-1] // group
    dv = v_ref.shape[-1] // group

    @pl.when(pl.program_id(2) == 0)
    def _():
        st_ref[...] = jnp.zeros_like(st_ref)
        op_ref[...] = jnp.zeros_like(op_ref)

    lvl = lvl_ref[...]
    tril = jnp.where(lvl >= 0, 1.0, 0.0).astype(BF16)
    row = lax.broadcasted_iota(jnp.int32, (c, dk), 0)
    heads = range(group)
    kcols = [slice(hh * dk, (hh + 1) * dk) for hh in heads]
    vcols = [slice(hh * dv, (hh + 1) * dv) for hh in heads]

    def chunk_rows(ci):
        return pl.ds(pl.multiple_of(ci * c, c), c)

    def cumulative_decay(rows, b_ref):
        for hh in heads:
            b_ref[hh] = (_dot(tril, g1_ref[rows, kcols[hh]])
                         + (_dot(tril, g2_ref[rows, kcols[hh]]) + _dot(tril, g3_ref[rows, kcols[hh]])))

    def finish(rows):
        for hh in heads:
            o = op_ref[hh]
            ms = jnp.mean(o * o, axis=-1, keepdims=True)
            on = o * lax.rsqrt(ms + NORM_EPS) * gt_ref[rows, vcols[hh]]
            o_ref[rows, vcols[hh]] = on.astype(o_ref.dtype)

    def chunk_step(ci, b_cur_ref, b_next_ref):
        rows = chunk_rows(ci)
        cumulative_decay(chunk_rows(jnp.minimum(ci + 1, n_chunks - 1)), b_next_ref)
        finish(chunk_rows(jnp.maximum(ci - 1, 0)))

        for hh in heads:
            q = q_ref[rows, kcols[hh]]
            k = k_ref[rows, kcols[hh]]
            b = b_cur_ref[hh]
            b_last = b_cur_ref[hh, pl.ds(c - 1, 1), :]
            st = st_ref[hh]
            op_ref[hh] = _dot_nt((q * jnp.exp2(b)).astype(BF16), st.astype(BF16))
            kl = (k * jnp.exp2(b_last - b)).astype(BF16)
            st_ref[hh] = st * jnp.exp2(b_last) + _dot_tn(v_ref[rows, vcols[hh]], kl)

        scores = []
        for hh in heads:
            q = q_ref[rows, kcols[hh]].astype(BF16)
            k = k_ref[rows, kcols[hh]].astype(BF16)
            b = b_cur_ref[hh]
            sc = jnp.where(lvl == 0, _dot_nt(q, k), 0.0)
            s = 1
            level = 1
            while s < c:
                e = jnp.exp2(_level_exponent(b, b_cur_ref.at[hh], sgn_ref, row, s, c)).astype(BF16)
                p = _dot_nt(q * e, k * e)
                sc = jnp.where(lvl == level, p, sc)
                s *= 2
                level += 1
            scores.append(sc.astype(BF16))
        for hh in heads:
            op_ref[hh] += _dot(scores[hh], v_ref[rows, vcols[hh]])

    def body(pair, carry):
        chunk_step(2 * pair, b0_ref, b1_ref)
        chunk_step(2 * pair + 1, b1_ref, b0_ref)
        return carry

    cumulative_decay(chunk_rows(0), b0_ref)
    lax.fori_loop(0, n_chunks // 2, body, 0)
    finish(chunk_rows(n_chunks - 1))


def _gla_chunk(q, k, g_split, v, gate, *, batch, heads, group, rows_per_step, chunk=CHUNK,
               cast_jobs=()):
    m, kd = q.shape
    vd = v.shape[1]
    dk, dv = kd // heads, vd // heads
    gk, gv = group * dk, group * dv
    seq = m // batch
    tc = min(rows_per_step, seq)
    steps = seq // tc
    assert tc % (2 * chunk) == 0
    kern = functools.partial(_gla_chunk_kernel, chunk=chunk, n_chunks=tc // chunk, group=group,
                             n_cast=len(cast_jobs))
    lvl = jnp.asarray(_pair_levels(chunk))
    sgn = jnp.asarray(_row_signs(chunk, dk))

    def rows_map(b, h, t):
        return b * steps + t, h

    declared = (2 * (3 * _nbytes((tc, gk), BF16) + 2 * _nbytes((tc, gk), F32)
                     + 2 * _nbytes((tc, gv), BF16) + _nbytes((tc, gv), F32))
                + _nbytes((group, dv, dk), F32) + 2 * _nbytes((group, chunk, dk), F32)
                + _nbytes((group, chunk, dv), F32))
    key_spec = pl.BlockSpec((tc, gk), rows_map)
    val_spec = pl.BlockSpec((tc, gv), rows_map)

    grid = (batch, heads // group, steps)
    n_steps = batch * (heads // group) * steps

    def flat_step(b, h, t):
        return (b * (heads // group) + h) * steps + t

    cast_in_specs, cast_out_specs, cast_out_shapes = [], [], []
    for w, layer, rows in cast_jobs:
        cols = w.shape[2]
        rb = rows // n_steps
        assert rb * n_steps == rows and rb % 16 == 0, (w.shape, n_steps)
        cast_in_specs.append(pl.BlockSpec(
            (None, rb, cols), functools.partial(lambda b, h, t, layer: (layer, flat_step(b, h, t), 0),
                                                layer=layer)))
        cast_out_specs.append(pl.BlockSpec((rb, cols), lambda b, h, t: (flat_step(b, h, t), 0)))
        cast_out_shapes.append(jax.ShapeDtypeStruct((rows, cols), BF16))
        declared += 2 * (_nbytes((rb, cols), F32) + _nbytes((rb, cols), BF16))

    outs = pl.pallas_call(
        kern,
        grid=grid,
        in_specs=[pl.BlockSpec(lvl.shape, lambda b, h, t: (0, 0)),
                  pl.BlockSpec(sgn.shape, lambda b, h, t: (0, 0, 0))] + [key_spec] * 5 + [val_spec] * 2
                 + cast_in_specs,
        out_specs=[val_spec] + cast_out_specs,
        out_shape=[jax.ShapeDtypeStruct((m, vd), BF16)] + cast_out_shapes,
        scratch_shapes=[pltpu.VMEM((group, dv, dk), F32), pltpu.VMEM((group, chunk, dk), F32),
                        pltpu.VMEM((group, chunk, dk), F32), pltpu.VMEM((group, chunk, dv), F32)],
        compiler_params=pltpu.CompilerParams(
            dimension_semantics=("arbitrary", "arbitrary", "arbitrary"),
            vmem_limit_bytes=_vmem_limit(declared)),
        name="gla_chunk",
    )(lvl, sgn, q, k, *g_split, v, gate, *[job[0] for job in cast_jobs])
    return outs[0], outs[1:]


def _outproj_kernel(o_ref, w_ref, h_ref, nw_ref, out_ref):
    mixed = _dot(o_ref[...], w_ref[...])
    out_ref[...] = h_ref[...] + _rms_scale(mixed, nw_ref[...])


def _outproj(o, w_out, h, norm_w, *, tm=512):
    m, d = h.shape
    dv = o.shape[1]
    declared = (2 * _nbytes((tm, dv), BF16) + 2 * _nbytes((dv, d), BF16) + 4 * _nbytes((tm, d), F32))
    return pl.pallas_call(
        _outproj_kernel,
        grid=(m // tm,),
        in_specs=[
            pl.BlockSpec((tm, dv), lambda i: (i, 0)),
            pl.BlockSpec((dv, d), lambda i: (0, 0)),
            pl.BlockSpec((tm, d), lambda i: (i, 0)),
            pl.BlockSpec((1, d), lambda i: (0, 0)),
        ],
        out_specs=pl.BlockSpec((tm, d), lambda i: (i, 0)),
        out_shape=jax.ShapeDtypeStruct((m, d), F32),
        compiler_params=pltpu.CompilerParams(
            dimension_semantics=("arbitrary",),
            vmem_limit_bytes=_vmem_limit(declared)),
        name="outproj",
    )(o, w_out, h, norm_w.reshape(1, d))


def _mlp_kernel(h_ref, pre_ref, wu_ref, wd_ref, post_ref, out_ref, xn_ref):
    kstep = pl.program_id(1)
    last = pl.num_programs(1) - 1
    tm = out_ref.shape[0]
    halves = [slice(r, r + tm // MLP_ROW_SPLIT) for r in range(0, tm, tm // MLP_ROW_SPLIT)]

    def hidden(xn):
        u = jnp.maximum(_dot(xn, wu_ref[...]), 0.0)
        return (u * u).astype(BF16)

    def down(u2, rows, add_to):
        for n in range(0, out_ref.shape[1], MLP_DOWN_COLS):
            cols = slice(n, n + MLP_DOWN_COLS)
            part = _dot(u2, wd_ref[:, cols])
            out_ref[rows, cols] = part if add_to is None else add_to[rows, cols] + part

    @pl.when(kstep == 0)
    def _():
        for rows in halves:
            xn = _rms_scale(h_ref[rows, :], pre_ref[...]).astype(BF16)
            xn_ref[rows, :] = xn
            down(hidden(xn), rows, None)

    @pl.when((kstep > 0) & (kstep < last))
    def _():
        down(hidden(xn_ref[...]), slice(None), out_ref)

    @pl.when(kstep == last)
    def _():
        for rows in halves:
            down(hidden(xn_ref[rows, :]), rows, out_ref)
            out_ref[rows, :] = h_ref[rows, :] + _rms_scale(out_ref[rows, :], post_ref[...])


def _mlp(h, pre_w, w_up, w_down, post_w, *, tm=1024, tk=1024):
    m, d = h.shape
    dff = w_up.shape[1]
    declared = (4 * _nbytes((tm, d), F32) + 2 * _nbytes((d, tk), BF16) + 2 * _nbytes((tk, d), BF16)
                + _nbytes((tm, d), BF16) + 2 * _nbytes((tm, tk), F32))
    return pl.pallas_call(
        _mlp_kernel,
        grid=(m // tm, dff // tk),
        in_specs=[
            pl.BlockSpec((tm, d), lambda i, k: (i, 0)),
            pl.BlockSpec((1, d), lambda i, k: (0, 0)),
            pl.BlockSpec((d, tk), lambda i, k: (0, k)),
            pl.BlockSpec((tk, d), lambda i, k: (k, 0)),
            pl.BlockSpec((1, d), lambda i, k: (0, 0)),
        ],
        out_specs=pl.BlockSpec((tm, d), lambda i, k: (i, 0)),
        out_shape=jax.ShapeDtypeStruct((m, d), F32),
        scratch_shapes=[pltpu.VMEM((tm, d), BF16)],
        compiler_params=pltpu.CompilerParams(
            dimension_semantics=("arbitrary", "arbitrary"),
            vmem_limit_bytes=_vmem_limit(declared)),
        name="mlp",
    )(h, pre_w.reshape(1, d), w_up, w_down, post_w.reshape(1, d))


def kernel(x, norm_mix_pre, norm_mix_post, norm_mlp_pre, norm_mlp_post, hgrn_w_in, hgrn_lb_logits, hgrn_norm, hgrn_w_out, gla_w_in, gla_w_gk, gla_b_gk, gla_norm, gla_w_out, mlp_w_up, mlp_w_down):
    batch, seq, d = x.shape
    depth = norm_mix_pre.shape[0]
    h = x.reshape(batch * seq, d)
    later = {}
    for layer in range(depth):
        j = layer // 2
        if layer % 2 == 0:
            if layer > 0:
                later["hgrn_in", j] = hgrn_w_in
            later["hgrn_out", j] = hgrn_w_out
        else:
            later["gla_in", j] = jnp.swapaxes(gla_w_in, 1, 2)
            later["gla_out", j] = gla_w_out
        later["mlp_up", layer] = mlp_w_up
        later["mlp_down", layer] = mlp_w_down
    bf16_w = {("hgrn_in", 0): hgrn_w_in[0].astype(BF16)}
    gla_rank = gla_w_gk.shape[1]
    gla_rows = gla_w_in.shape[2] - gla_rank

    for layer in range(depth):
        j = layer // 2
        cast_jobs = [(w, key[1], gla_rows if key[0] == "gla_in" else w.shape[1])
                     for key, w in later.items()] if layer == 0 else []
        if layer % 2 == 0:
            gain_row = jnp.tile(hgrn_norm[j], HGRN_HEADS).reshape(1, -1)
            q, k, g1, g2, g3, v, gate = _hgrn_inproj(h, norm_mix_pre[layer], bf16_w["hgrn_in", j],
                                                     hgrn_lb_logits, gain_row, layer)
            o, cast = _gla_chunk(q, k, (g1, g2, g3), v, gate, batch=batch, heads=HGRN_HEADS,
                                 group=16, rows_per_step=256, cast_jobs=cast_jobs)
            w_out = ("hgrn_out", j)
        else:
            key_dim = gla_w_gk.shape[2]
            rank = gla_w_gk.shape[1]
            value_dim = (gla_w_in.shape[2] - 2 * key_dim - rank) // 2
            n_main = 2 * key_dim + 2 * value_dim
            w_r = jnp.pad(gla_w_in[j, :, n_main:], ((0, 0), (0, LANES - rank))).astype(BF16)
            w_gk = jnp.pad(gla_w_gk[j], ((0, LANES - rank), (0, 0)))
            gain_row = jnp.tile(gla_norm[j], GLA_HEADS).reshape(1, -1)
            q, k, g1, g2, g3, v, gate = _gla_inproj(h, norm_mix_pre[layer], bf16_w["gla_in", j], w_r,
                                                    w_gk, gla_b_gk[j], gain_row,
                                                    key_dim=key_dim, value_dim=value_dim, heads=GLA_HEADS)
            o, cast = _gla_chunk(q, k, (g1, g2, g3), v, gate, batch=batch, heads=GLA_HEADS,
                                 group=4, rows_per_step=512, chunk=128, cast_jobs=cast_jobs)
            w_out = ("gla_out", j)
        if layer == 0:
            bf16_w.update(zip(later.keys(), cast))
        h = _outproj(o, bf16_w[w_out], h, norm_mix_post[layer])
        h = _mlp(h, norm_mlp_pre[layer], bf16_w["mlp_up", layer], bf16_w["mlp_down", layer],
                 norm_mlp_post[layer])
    return h.reshape(batch, seq, d)
```

```python
import functools
import math

import numpy as np
import jax
import jax.numpy as jnp
from jax import lax
from jax.experimental import pallas as pl
from jax.experimental.pallas import tpu as pltpu

F32 = jnp.float32
BF16 = jnp.bfloat16

NORM_EPS = 1e-6
CHUNK = 64
HGRN_HEADS = 16
GLA_HEADS = 4
GLA_GATE_NORMALIZER = 16.0
LOG2_E = math.log2(math.e)

V7X_VMEM_BYTES = 64 * 1024 * 1024
VMEM_RESERVED_BYTES = 2 * 1024 * 1024
VMEM_HEADROOM_BYTES = 6 * 1024 * 1024
LANES = 128
SUBLANES = 8
SMALL_LEVELS = tuple(2 ** i for i in range(int(math.log2(SUBLANES))))
MLP_DOWN_COLS = 512
MLP_ROW_SPLIT = 2


def _vmem_limit(declared_bytes):
    return int(min(V7X_VMEM_BYTES - VMEM_RESERVED_BYTES, declared_bytes + VMEM_HEADROOM_BYTES))


def _nbytes(shape, dtype):
    return int(np.prod(shape)) * jnp.dtype(dtype).itemsize


def _dot(a, b):
    return jnp.dot(a, b, preferred_element_type=F32)


def _dot_nt(a, b):
    return lax.dot_general(a, b, (((1,), (1,)), ((), ())), preferred_element_type=F32)


def _dot_tn(a, b):
    return lax.dot_general(a, b, (((0,), (0,)), ((), ())), preferred_element_type=F32)


def _sigmoid(x):
    return 0.5 * jnp.tanh(0.5 * x) + 0.5


def _silu(x):
    return x * _sigmoid(x)


def _rms_scale(x, w):
    ms = jnp.mean(x * x, axis=-1, keepdims=True)
    return x * lax.rsqrt(ms + NORM_EPS) * w


def _split3_bf16(x):
    a = x.astype(BF16)
    r = x - a.astype(F32)
    b = r.astype(BF16)
    return a, b, (r - b.astype(F32)).astype(BF16)


def _project_row_tile(x_ref, nw_ref, xn_ref, project, row_split):
    first = pl.program_id(1) == 0
    tm = x_ref.shape[0]

    @pl.when(first)
    def _():
        for r in range(0, tm, tm // row_split):
            rows = slice(r, r + tm // row_split)
            xn = _rms_scale(x_ref[rows, :], nw_ref[...]).astype(BF16)
            xn_ref[rows, :] = xn
            project(xn, rows, True)

    @pl.when(jnp.logical_not(first))
    def _():
        project(xn_ref[...], slice(None), False)


def _hgrn_inproj_kernel(x_ref, nw_ref, wq_ref, wf_ref, wi_ref, wg_ref, lbl_ref, gain_ref,
                        q_ref, k_ref, g1_ref, g2_ref, g3_ref, v_ref, gt_ref, xn_ref, *, layer, row_split):
    def project(xn, rows, first_step):
        lg = lbl_ref[...]
        e = jnp.exp(lg - jnp.max(lg, axis=0, keepdims=True))
        lb = jnp.sum(e[:layer + 1], axis=0, keepdims=True) / jnp.sum(e, axis=0, keepdims=True)
        forget = lb + (1.0 - lb) * _sigmoid(_dot(xn, wf_ref[...]))
        k_ref[rows, :] = 1.0 - forget
        g1_ref[rows, :], g2_ref[rows, :], g3_ref[rows, :] = _split3_bf16(jnp.log2(forget))
        gt_ref[rows, :] = _silu(_dot(xn, wg_ref[...])) * gain_ref[...]
        q_ref[rows, :] = _silu(_dot(xn, wq_ref[...]))
        v_ref[rows, :] = _dot(xn, wi_ref[...]).astype(BF16)

    _project_row_tile(x_ref, nw_ref, xn_ref, project, row_split)


def _hgrn_inproj(h, norm_w, w_in, lb_logits, gain_row, layer, *, tm=1024, tn=512):
    m, d = h.shape
    df = w_in.shape[1] // 4
    tiles = df // tn
    kern = functools.partial(_hgrn_inproj_kernel, layer=layer, row_split=2)
    bf_out = jax.ShapeDtypeStruct((m, df), BF16)
    f32_out = jax.ShapeDtypeStruct((m, df), F32)
    declared = (2 * _nbytes((tm, d), F32) + _nbytes((tm, d), BF16) + 8 * _nbytes((d, tn), BF16)
                + 2 * (4 * _nbytes((tm, tn), BF16) + 3 * _nbytes((tm, tn), F32)))
    w_spec =[pl.BlockSpec((d, tn), functools.partial(lambda i, j, g: (0, g * tiles + j), g=g))
              for g in range(4)]
    out_spec = pl.BlockSpec((tm, tn), lambda i, j: (i, j))
    return pl.pallas_call(
        kern,
        grid=(m // tm, tiles),
        in_specs=[
            pl.BlockSpec((tm, d), lambda i, j: (i, 0)),
            pl.BlockSpec((1, d), lambda i, j: (0, 0)),
            *w_spec,
            pl.BlockSpec((lb_logits.shape[0], tn), lambda i, j: (0, j)),
            pl.BlockSpec((1, tn), lambda i, j: (0, j)),
        ],
        out_specs=[out_spec] * 7,
        out_shape=[f32_out] * 2 + [bf_out] * 4 + [f32_out],
        scratch_shapes=[pltpu.VMEM((tm, d), BF16)],
        compiler_params=pltpu.CompilerParams(
            dimension_semantics=("arbitrary", "arbitrary"),
            vmem_limit_bytes=_vmem_limit(declared)),
        name="hgrn_inproj",
    )(h, norm_w.reshape(1, d), w_in, w_in, w_in, w_in, lb_logits, gain_row)


def _gla_inproj_kernel(x_ref, nw_ref, wq_ref, wk_ref, wv_ref, wg_ref, wr_ref, wgk_ref, bgk_ref,
                       gain_ref, q_ref, k_ref, g1_ref, g2_ref, g3_ref, v_ref, gt_ref,
                       xn_ref, r_ref, *, q_scale, row_split):
    def project(xn, rows, first_step):
        if first_step:
            r_ref[rows, :] = _dot(xn, wr_ref[...])
        r = r_ref[rows, :]
        w = wgk_ref[...]
        r_hi = r.astype(BF16)
        r_lo = (r - r_hi.astype(F32)).astype(BF16)
        w_hi = w.astype(BF16)
        w_lo = (w - w_hi.astype(F32)).astype(BF16)
        z = _dot(r_hi, w_hi) + (_dot(r_hi, w_lo) + _dot(r_lo, w_hi)) + bgk_ref[...]
        log_sig = jnp.minimum(z, 0.0) - jnp.log(1.0 + jnp.exp(-jnp.abs(z)))
        g1_ref[rows, :], g2_ref[rows, :], g3_ref[rows, :] = _split3_bf16(
            log_sig * (LOG2_E / GLA_GATE_NORMALIZER))
        gt_ref[rows, :] = _silu(_dot_nt(xn, wg_ref[...])) * gain_ref[...]
        q_ref[rows, :] = _dot_nt(xn, wq_ref[...]) * q_scale
        k_ref[rows, :] = _dot_nt(xn, wk_ref[...])
        v_ref[rows, :] = _dot_nt(xn, wv_ref[...]).astype(BF16)

    _project_row_tile(x_ref, nw_ref, xn_ref, project, row_split)


def _gla_inproj(h, norm_w, w_in_t, w_r, w_gk, b_gk, gain_row, *, key_dim, value_dim, heads,
                tm=1024, steps=4):
    m, d = h.shape
    tk, tv = key_dim // steps, value_dim // steps
    kern = functools.partial(_gla_inproj_kernel, q_scale=float((key_dim // heads) ** -0.5),
                             row_split=1)
    declared = (2 * _nbytes((tm, d), F32) + _nbytes((tm, d), BF16)
                + 4 * _nbytes((d, tk), BF16) + 4 * _nbytes((d, tv), BF16)
                + 2 * _nbytes((d, LANES), BF16) + _nbytes((tm, LANES), F32)
                + 2 * (3 * _nbytes((tm, tk), BF16) + 2 * _nbytes((tm, tk), F32)
                       + _nbytes((tm, tv), F32) + _nbytes((tm, tv), BF16)))
    k_first, v_first, g_first = key_dim // tk, 2 * key_dim // tv, (2 * key_dim + value_dim) // tv
    key_out = pl.BlockSpec((tm, tk), lambda i, j: (i, j))
    val_out = pl.BlockSpec((tm, tv), lambda i, j: (i, j))
    key_bf = jax.ShapeDtypeStruct((m, key_dim), BF16)
    key_f32 = jax.ShapeDtypeStruct((m, key_dim), F32)
    return pl.pallas_call(
        kern,
        grid=(m // tm, steps),
        in_specs=[
            pl.BlockSpec((tm, d), lambda i, j: (i, 0)),
            pl.BlockSpec((1, d), lambda i, j: (0, 0)),
            pl.BlockSpec((tk, d), lambda i, j: (j, 0)),
            pl.BlockSpec((tk, d), lambda i, j: (k_first + j, 0)),
            pl.BlockSpec((tv, d), lambda i, j: (v_first + j, 0)),
            pl.BlockSpec((tv, d), lambda i, j: (g_first + j, 0)),
            pl.BlockSpec((d, LANES), lambda i, j: (0, 0)),
            pl.BlockSpec((LANES, tk), lambda i, j: (0, j)),
            pl.BlockSpec((1, tk), lambda i, j: (0, j)),
            pl.BlockSpec((1, tv), lambda i, j: (0, j)),
        ],
        out_specs=[key_out] * 5 + [val_out, val_out],
        out_shape=[key_f32] * 2 + [key_bf] * 3 + [jax.ShapeDtypeStruct((m, value_dim), BF16),
                                     jax.ShapeDtypeStruct((m, value_dim), F32)],
        scratch_shapes=[pltpu.VMEM((tm, d), BF16), pltpu.VMEM((tm, LANES), F32)],
        compiler_params=pltpu.CompilerParams(
            dimension_semantics=("arbitrary", "arbitrary"),
            vmem_limit_bytes=_vmem_limit(declared)),
        name="gla_inproj",
    )(h, norm_w.reshape(1, d), w_in_t, w_in_t, w_in_t, w_in_t, w_r, w_gk, b_gk.reshape(1, key_dim), gain_row)


def _pair_levels(c):
    i = np.arange(c)[:, None]
    j = np.arange(c)[None, :]
    x = i ^ j
    lvl = np.where(x > 0, np.floor(np.log2(np.maximum(x, 1))).astype(np.int64) + 1, 0)
    return np.where(j > i, -1, lvl).astype(np.int32)


def _row_signs(c, dk):
    r = np.arange(c)[None, :, None]
    s = np.array(SMALL_LEVELS)[:, None, None]
    return np.broadcast_to(np.where(r & s, 1.0, -1.0), (len(SMALL_LEVELS), c, dk)).astype(np.float32)


def _reference_rows(b, b_ref, row, s, c):
    dk = b.shape[-1]
    if s == 1:
        return jnp.where((row & 1) != 0, pltpu.roll(b, 1, 0), b)
    if s == 2:
        lo = [jnp.broadcast_to(b_ref[pl.ds(8 * v + 1, 1), :], (8, dk)) for v in range(c // 8)]
        hi = [jnp.broadcast_to(b_ref[pl.ds(8 * v + 5, 1), :], (8, dk)) for v in range(c // 8)]
        return jnp.where((row & 4) != 0, jnp.concatenate(hi, axis=0), jnp.concatenate(lo, axis=0))
    parts = [jnp.broadcast_to(b_ref[pl.ds(m * 2 * s + s - 1, 1), :], (2 * s, dk))
             for m in range(c // (2 * s))]
    return parts[0] if len(parts) == 1 else jnp.concatenate(parts, axis=0)


def _level_exponent(b, b_ref, sgn_ref, row, s, c):
    dk = b.shape[-1]
    if s in SMALL_LEVELS:
        return (b - _reference_rows(b, b_ref, row, s, c)) * sgn_ref[SMALL_LEVELS.index(s)]
    parts = []
    for m in range(c // (2 * s)):
        lo = m * 2 * s
        r = jnp.broadcast_to(b_ref[pl.ds(lo + s - 1, 1), :], (s, dk))
        parts += [r - b[lo:lo + s], b[lo + s:lo + 2 * s] - r]
    return jnp.concatenate(parts, axis=0)


def _gla_chunk_kernel(*refs, chunk, n_chunks, group, n_cast):
    lvl_ref, sgn_ref, q_ref, k_ref, g1_ref, g2_ref, g3_ref, v_ref, gt_ref = refs[:9]
    cast_src = refs[9:9 + n_cast]
    o_ref = refs[9 + n_cast]
    cast_dst = refs[10 + n_cast:10 + 2 * n_cast]
    st_ref, b0_ref, b1_ref, op_ref = refs[10 + 2 * n_cast:]
    c = chunk

    for src, dst in zip(cast_src, cast_dst):
        dst[...] = src[...].astype(dst.dtype)
    dk = q_ref.shape[-1] // group
    dv = v_ref.shape[-1] // group

    @pl.when(pl.program_id(2) == 0)
    def _():
        st_ref[...] = jnp.zeros_like(st_ref)
        op_ref[...] = jnp.zeros_like(op_ref)

    lvl = lvl_ref[...]
    tril = jnp.where(lvl >= 0, 1.0, 0.0).astype(BF16)
    row = lax.broadcasted_iota(jnp.int32, (c, dk), 0)
    heads = range(group)
    kcols = [slice(hh * dk, (hh + 1) * dk) for hh in heads]
    vcols = [slice(hh * dv, (hh + 1) * dv) for hh in heads]

    def chunk_rows(ci):
        return pl.ds(pl.multiple_of(ci * c, c), c)

    def cumulative_decay(rows, b_ref):
        for hh in heads:
            b_ref[hh] = (_dot(tril, g1_ref[rows, kcols[hh]])
                         + (_dot(tril, g2_ref[rows, kcols[hh]]) + _dot(tril, g3_ref[rows, kcols[hh]])))

    def finish(rows):
        for hh in heads:
            o = op_ref[hh]
            ms = jnp.mean(o * o, axis=-1, keepdims=True)
            on = o * lax.rsqrt(ms + NORM_EPS) * gt_ref[rows, vcols[hh]]
            o_ref[rows, vcols[hh]] = on.astype(o_ref.dtype)

    def chunk_step(ci, b_cur_ref, b_next_ref):
        rows = chunk_rows(ci)
        cumulative_decay(chunk_rows(jnp.minimum(ci + 1, n_chunks - 1)), b_next_ref)
        finish(chunk_rows(jnp.maximum(ci - 1, 0)))

        for hh in heads:
            q = q_ref[rows, kcols[hh]]
            k = k_ref[rows, kcols[hh]]
            b = b_cur_ref[hh]
            b_last = b_cur_ref[hh, pl.ds(c - 1, 1), :]
            st = st_ref[hh]
            op_ref[hh] = _dot_nt((q * jnp.exp2(b)).astype(BF16), st.astype(BF16))
            kl = (k * jnp.exp2(b_last - b)).astype(BF16)
            st_ref[hh] = st * jnp.exp2(b_last) + _dot_tn(v_ref[rows, vcols[hh]], kl)

        scores = []
        for hh in heads:
            q = q_ref[rows, kcols[hh]].astype(BF16)
            k = k_ref[rows, kcols[hh]].astype(BF16)
            b = b_cur_ref[hh]
            sc = jnp.where(lvl == 0, _dot_nt(q, k), 0.0)
            s = 1
            level = 1
            while s < c:
                e = jnp.exp2(_level_exponent(b, b_cur_ref.at[hh], sgn_ref, row, s, c)).astype(BF16)
                p = _dot_nt(q * e, k * e)
                sc = jnp.where(lvl == level, p, sc)
                s *= 2
                level += 1
            scores.append(sc.astype(BF16))
        for hh in heads:
            op_ref[hh] += _dot(scores[hh], v_ref[rows, vcols[hh]])

    def body(pair, carry):
        chunk_step(2 * pair, b0_ref, b1_ref)
        chunk_step(2 * pair + 1, b1_ref, b0_ref)
        return carry

    cumulative_decay(chunk_rows(0), b0_ref)
    lax.fori_loop(0, n_chunks // 2, body, 0)
    finish(chunk_rows(n_chunks - 1))


def _gla_chunk(q, k, g_split, v, gate, *, batch, heads, group, rows_per_step, chunk=CHUNK,
               cast_jobs=()):
    m, kd = q.shape
    vd = v.shape[1]
    dk, dv = kd // heads, vd // heads
    gk, gv = group * dk, group * dv
    seq = m // batch
    tc = min(rows_per_step, seq)
    steps = seq // tc
    assert tc % (2 * chunk) == 0
    kern = functools.partial(_gla_chunk_kernel, chunk=chunk, n_chunks=tc // chunk, group=group,
                             n_cast=len(cast_jobs))
    lvl = jnp.asarray(_pair_levels(chunk))
    sgn = jnp.asarray(_row_signs(chunk, dk))

    def rows_map(b, h, t):
        return b * steps + t, h

    declared = (2 * (3 * _nbytes((tc, gk), BF16) + 2 * _nbytes((tc, gk), F32)
                     + 2 * _nbytes((tc, gv), BF16) + _nbytes((tc, gv), F32))
                + _nbytes((group, dv, dk), F32) + 2 * _nbytes((group, chunk, dk), F32)
                + _nbytes((group, chunk, dv), F32))
    key_spec = pl.BlockSpec((tc, gk), rows_map)
    val_spec = pl.BlockSpec((tc, gv), rows_map)

    grid = (batch, heads // group, steps)
    n_steps = batch * (heads // group) * steps

    def flat_step(b, h, t):
        return (b * (heads // group) + h) * steps + t

    cast_in_specs, cast_out_specs, cast_out_shapes = [], [], []
    for w, layer, rows in cast_jobs:
        cols = w.shape[2]
        rb = rows // n_steps
        assert rb * n_steps == rows and rb % 16 == 0, (w.shape, n_steps)
        cast_in_specs.append(pl.BlockSpec(
            (None, rb, cols), functools.partial(lambda b, h, t, layer: (layer, flat_step(b, h, t), 0),
                                                layer=layer)))
        cast_out_specs.append(pl.BlockSpec((rb, cols), lambda b, h, t: (flat_step(b, h, t), 0)))
        cast_out_shapes.append(jax.ShapeDtypeStruct((rows, cols), BF16))
        declared += 2 * (_nbytes((rb, cols), F32) + _nbytes((rb, cols), BF16))

    outs = pl.pallas_call(
        kern,
        grid=grid,
        in_specs=[pl.BlockSpec(lvl.shape, lambda b, h, t: (0, 0)),
                  pl.BlockSpec(sgn.shape, lambda b, h, t: (0, 0, 0))] + [key_spec] * 5 + [val_spec] * 2
                 + cast_in_specs,
        out_specs=[val_spec] + cast_out_specs,
        out_shape=[jax.ShapeDtypeStruct((m, vd), BF16)] + cast_out_shapes,
        scratch_shapes=[pltpu.VMEM((group, dv, dk), F32), pltpu.VMEM((group, chunk, dk), F32),
                        pltpu.VMEM((group, chunk, dk), F32), pltpu.VMEM((group, chunk, dv), F32)],
        compiler_params=pltpu.CompilerParams(
            dimension_semantics=("arbitrary", "arbitrary", "arbitrary"),
            vmem_limit_bytes=_vmem_limit(declared)),
        name="gla_chunk",
    )(lvl, sgn, q, k, *g_split, v, gate, *[job[0] for job in cast_jobs])
    return outs[0], outs[1:]


def _outproj_kernel(o_ref, w_ref, h_ref, nw_ref, out_ref):
    mixed = _dot(o_ref[...], w_ref[...])
    out_ref[...] = h_ref[...] + _rms_scale(mixed, nw_ref[...])


def _outproj(o, w_out, h, norm_w, *, tm=512):
    m, d = h.shape
    dv = o.shape[1]
    declared = (2 * _nbytes((tm, dv), BF16) + 2 * _nbytes((dv, d), BF16) + 4 * _nbytes((tm, d), F32))
    return pl.pallas_call(
        _outproj_kernel,
        grid=(m // tm,),
        in_specs=[
            pl.BlockSpec((tm, dv), lambda i: (i, 0)),
            pl.BlockSpec((dv, d), lambda i: (0, 0)),
            pl.BlockSpec((tm, d), lambda i: (i, 0)),
            pl.BlockSpec((1, d), lambda i: (0, 0)),
        ],
        out_specs=pl.BlockSpec((tm, d), lambda i: (i, 0)),
        out_shape=jax.ShapeDtypeStruct((m, d), F32),
        compiler_params=pltpu.CompilerParams(
            dimension_semantics=("arbitrary",),
            vmem_limit_bytes=_vmem_limit(declared)),
        name="outproj",
    )(o, w_out, h, norm_w.reshape(1, d))


def _mlp_kernel(h_ref, pre_ref, wu_ref, wd_ref, post_ref, out_ref, xn_ref):
    kstep = pl.program_id(1)
    last = pl.num_programs(1) - 1
    tm = out_ref.shape[0]
    halves = [slice(r, r + tm // MLP_ROW_SPLIT) for r in range(0, tm, tm // MLP_ROW_SPLIT)]

    def hidden(xn):
        u = jnp.maximum(_dot(xn, wu_ref[...]), 0.0)
        return (u * u).astype(BF16)

    def down(u2, rows, add_to):
        for n in range(0, out_ref.shape[1], MLP_DOWN_COLS):
            cols = slice(n, n + MLP_DOWN_COLS)
            part = _dot(u2, wd_ref[:, cols])
            out_ref[rows, cols] = part if add_to is None else add_to[rows, cols] + part

    @pl.when(kstep == 0)
    def _():
        for rows in halves:
            xn = _rms_scale(h_ref[rows, :], pre_ref[...]).astype(BF16)
            xn_ref[rows, :] = xn
            down(hidden(xn), rows, None)

    @pl.when((kstep > 0) & (kstep < last))
    def _():
        down(hidden(xn_ref[...]), slice(None), out_ref)

    @pl.when(kstep == last)
    def _():
        for rows in halves:
            down(hidden(xn_ref[rows, :]), rows, out_ref)
            out_ref[rows, :] = h_ref[rows, :] + _rms_scale(out_ref[rows, :], post_ref[...])


def _mlp(h, pre_w, w_up, w_down, post_w, *, tm=1024, tk=1024):
    m, d = h.shape
    dff = w_up.shape[1]
    declared = (4 * _nbytes((tm, d), F32) + 2 * _nbytes((d, tk), BF16) + 2 * _nbytes((tk, d), BF16)
                + _nbytes((tm, d), BF16) + 2 * _nbytes((tm, tk), F32))
    return pl.pallas_call(
        _mlp_kernel,
        grid=(m // tm, dff // tk),
        in_specs=[
            pl.BlockSpec((tm, d), lambda i, k: (i, 0)),
            pl.BlockSpec((1, d), lambda i, k: (0, 0)),
            pl.BlockSpec((d, tk), lambda i, k: (0, k)),
            pl.BlockSpec((tk, d), lambda i, k: (k, 0)),
            pl.BlockSpec((1, d), lambda i, k: (0, 0)),
        ],
        out_specs=pl.BlockSpec((tm, d), lambda i, k: (i, 0)),
        out_shape=jax.ShapeDtypeStruct((m, d), F32),
        scratch_shapes=[pltpu.VMEM((tm, d), BF16)],
        compiler_params=pltpu.CompilerParams(
            dimension_semantics=("arbitrary", "arbitrary"),
            vmem_limit_bytes=_vmem_limit(declared)),
        name="mlp",
    )(h, pre_w.reshape(1, d), w_up, w_down, post_w.reshape(1, d))


def kernel(x, norm_mix_pre, norm_mix_post, norm_mlp_pre, norm_mlp_post, hgrn_w_in, hgrn_lb_logits, hgrn_norm, hgrn_w_out, gla_w_in, gla_w_gk, gla_b_gk, gla_norm, gla_w_out, mlp_w_up, mlp_w_down):
    batch, seq, d = x.shape
    depth = norm_mix_pre.shape[0]
    h = x.reshape(batch * seq, d)
    later = {}
    for layer in range(depth):
        j = layer // 2
        if layer % 2 == 0:
            if layer > 0:
                later["hgrn_in", j] = hgrn_w_in
            later["hgrn_out", j] = hgrn_w_out
        else:
            later["gla_in", j] = jnp.swapaxes(gla_w_in, 1, 2)
            later["gla_out", j] = gla_w_out
        later["mlp_up", layer] = mlp_w_up
        later["mlp_down", layer] = mlp_w_down
    bf16_w = {("hgrn_in", 0): hgrn_w_in[0].astype(BF16)}
    gla_rank = gla_w_gk.shape[1]
    gla_rows = gla_w_in.shape[2] - gla_rank

    for layer in range(depth):
        j = layer // 2
        cast_jobs = [(w, key[1], gla_rows if key[0] == "gla_in" else w.shape[1])
                     for key, w in later.items()] if layer == 0 else []
        if layer % 2 == 0:
            gain_row = jnp.tile(hgrn_norm[j], HGRN_HEADS).reshape(1, -1)
            q, k, g1, g2, g3, v, gate = _hgrn_inproj(h, norm_mix_pre[layer], bf16_w["hgrn_in", j],
                                                     hgrn_lb_logits, gain_row, layer)
            o, cast = _gla_chunk(q, k, (g1, g2, g3), v, gate, batch=batch, heads=HGRN_HEADS,
                                 group=16, rows_per_step=256, cast_jobs=cast_jobs)
            w_out = ("hgrn_out", j)
        else:
            key_dim = gla_w_gk.shape[2]
            rank = gla_w_gk.shape[1]
            value_dim = (gla_w_in.shape[2] - 2 * key_dim - rank) // 2
            n_main = 2 * key_dim + 2 * value_dim
            w_r = jnp.pad(gla_w_in[j, :, n_main:], ((0, 0), (0, LANES - rank))).astype(BF16)
            w_gk = jnp.pad(gla_w_gk[j], ((0, LANES - rank), (0, 0)))
            gain_row = jnp.tile(gla_norm[j], GLA_HEADS).reshape(1, -1)
            q, k, g1, g2, g3, v, gate = _gla_inproj(h, norm_mix_pre[layer], bf16_w["gla_in", j], w_r,
                                                    w_gk, gla_b_gk[j], gain_row,
                                                    key_dim=key_dim, value_dim=value_dim, heads=GLA_HEADS)
            o, cast = _gla_chunk(q, k, (g1, g2, g3), v, gate, batch=batch, heads=GLA_HEADS,
                                 group=4, rows_per_step=512, chunk=128, cast_jobs=cast_jobs)
            w_out = ("gla_out", j)
        if layer == 0:
            bf16_w.update(zip(later.keys(), cast))
        h = _outproj(o, bf16_w[w_out], h, norm_mix_post[layer])
        h = _mlp(h, norm_mlp_pre[layer], bf16_w["mlp_up", layer], bf16_w["mlp_down", layer],
                 norm_mlp_post[layer])
    return h.reshape(batch, seq, d)
```

```python
import functools
import math

import numpy as np
import jax
import jax.numpy as jnp
from jax import lax
from jax.experimental import pallas as pl
from jax.experimental.pallas import tpu as pltpu

F32 = jnp.float32
BF16 = jnp.bfloat16

NORM_EPS = 1e-6
CHUNK = 64
HGRN_HEADS = 16
GLA_HEADS = 4
GLA_GATE_NORMALIZER = 16.0
LOG2_E = math.log2(math.e)

V7X_VMEM_BYTES = 64 * 1024 * 1024
VMEM_RESERVED_BYTES = 2 * 1024 * 1024
VMEM_HEADROOM_BYTES = 6 * 1024 * 1024
LANES = 128
SUBLANES = 8
SMALL_LEVELS = tuple(2 ** i for i in range(int(math.log2(SUBLANES))))
MLP_DOWN_COLS = 512
MLP_ROW_SPLIT = 2


def _vmem_limit(declared_bytes):
    return int(min(V7X_VMEM_BYTES - VMEM_RESERVED_BYTES, declared_bytes + VMEM_HEADROOM_BYTES))


def _nbytes(shape, dtype):
    return int(np.prod(shape)) * jnp.dtype(dtype).itemsize


def _dot(a, b):
    return jnp.dot(a, b, preferred_element_type=F32)


def _dot_nt(a, b):
    return lax.dot_general(a, b, (((1,), (1,)), ((), ())), preferred_element_type=F32)


def _dot_tn(a, b):
    return lax.dot_general(a, b, (((0,), (0,)), ((), ())), preferred_element_type=F32)


def _sigmoid(x):
    return 0.5 * jnp.tanh(0.5 * x) + 0.5


def _silu(x):
    return x * _sigmoid(x)


def _rms_scale(x, w):
    ms = jnp.mean(x * x, axis=-1, keepdims=True)
    return x * lax.rsqrt(ms + NORM_EPS) * w


def _split3_bf16(x):
    a = x.astype(BF16)
    r = x - a.astype(F32)
    b = r.astype(BF16)
    return a, b, (r - b.astype(F32)).astype(BF16)


def _project_row_tile(x_ref, nw_ref, xn_ref, project, row_split):
    first = pl.program_id(1) == 0
    tm = x_ref.shape[0]

    @pl.when(first)
    def _():
        for r in range(0, tm, tm // row_split):
            rows = slice(r, r + tm // row_split)
            xn = _rms_scale(x_ref[rows, :], nw_ref[...]).astype(BF16)
            xn_ref[rows, :] = xn
            project(xn, rows, True)

    @pl.when(jnp.logical_not(first))
    def _():
        project(xn_ref[...], slice(None), False)


def _hgrn_inproj_kernel(x_ref, nw_ref, wq_ref, wf_ref, wi_ref, wg_ref, lbl_ref, gain_ref,
                        q_ref, k_ref, g1_ref, g2_ref, g3_ref, v_ref, gt_ref, xn_ref, *, layer, row_split):
    def project(xn, rows, first_step):
        lg = lbl_ref[...]
        e = jnp.exp(lg - jnp.max(lg, axis=0, keepdims=True))
        lb = jnp.sum(e[:layer + 1], axis=0, keepdims=True) / jnp.sum(e, axis=0, keepdims=True)
        forget = lb + (1.0 - lb) * _sigmoid(_dot(xn, wf_ref[...]))
        k_ref[rows, :] = 1.0 - forget
        g1_ref[rows, :], g2_ref[rows, :], g3_ref[rows, :] = _split3_bf16(jnp.log2(forget))
        gt_ref[rows, :] = _silu(_dot(xn, wg_ref[...])) * gain_ref[...]
        q_ref[rows, :] = _silu(_dot(xn, wq_ref[...]))
        v_ref[rows, :] = _dot(xn, wi_ref[...]).astype(BF16)

    _project_row_tile(x_ref, nw_ref, xn_ref, project, row_split)


def _hgrn_inproj(h, norm_w, w_in, lb_logits, gain_row, layer, *, tm=1024, tn=512):
    m, d = h.shape
    df = w_in.shape[1] // 4
    tiles = df // tn
    kern = functools.partial(_hgrn_inproj_kernel, layer=layer, row_split=2)
    bf_out = jax.ShapeDtypeStruct((m, df), BF16)
    f32_out = jax.ShapeDtypeStruct((m, df), F32)
    declared = (2 * _nbytes((tm, d), F32) + _nbytes((tm, d), BF16) + 8 * _nbytes((d, tn), BF16)
                + 2 * (4 * _nbytes((tm, tn), BF16) + 3 * _nbytes((tm, tn), F32)))
    w_spec =[pl.BlockSpec((d, tn), functools.partial(lambda i, j, g: (0, g * tiles + j), g=g))
              for g in range(4)]
    out_spec = pl.BlockSpec((tm, tn), lambda i, j: (i, j))
    return pl.pallas_call(
        kern,
        grid=(m // tm, tiles),
        in_specs=[
            pl.BlockSpec((tm, d), lambda i, j: (i, 0)),
            pl.BlockSpec((1, d), lambda i, j: (0, 0)),
            *w_spec,
            pl.BlockSpec((lb_logits.shape[0], tn), lambda i, j: (0, j)),
            pl.BlockSpec((1, tn), lambda i, j: (0, j)),
        ],
        out_specs=[out_spec] * 7,
        out_shape=[f32_out] * 2 + [bf_out] * 4 + [f32_out],
        scratch_shapes=[pltpu.VMEM((tm, d), BF16)],
        compiler_params=pltpu.CompilerParams(
            dimension_semantics=("arbitrary", "arbitrary"),
            vmem_limit_bytes=_vmem_limit(declared)),
        name="hgrn_inproj",
    )(h, norm_w.reshape(1, d), w_in, w_in, w_in, w_in, lb_logits, gain_row)


def _gla_inproj_kernel(x_ref, nw_ref, wq_ref, wk_ref, wv_ref, wg_ref, wr_ref, wgk_ref, bgk_ref,
                       gain_ref, q_ref, k_ref, g1_ref, g2_ref, g3_ref, v_ref, gt_ref,
                       xn_ref, r_ref, *, q_scale, rank, row_split):
    def project(xn, rows, first_step):
        if first_step:
            r = _dot(xn, wr_ref[...])
            head = r.astype(BF16).astype(F32)
            lane = lax.broadcasted_iota(jnp.int32, r.shape, 1)
            middle = (lane >= rank) & (lane < 2 * rank)
            r_ref[rows, :] = jnp.where(middle, r - head, head).astype(BF16)
        z = _dot(r_ref[rows, :], wgk_ref[...]) + bgk_ref[...]
        log_sig = jnp.minimum(z, 0.0) - jnp.log(1.0 + jnp.exp(-jnp.abs(z)))
        g1_ref[rows, :], g2_ref[rows, :], g3_ref[rows, :] = _split3_bf16(
            log_sig * (LOG2_E / GLA_GATE_NORMALIZER))
        gt_ref[rows, :] = _silu(_dot_nt(xn, wg_ref[...])) * gain_ref[...]
        q_ref[rows, :] = _dot_nt(xn, wq_ref[...]) * q_scale
        k_ref[rows, :] = _dot_nt(xn, wk_ref[...])
        v_ref[rows, :] = _dot_nt(xn, wv_ref[...]).astype(BF16)

    _project_row_tile(x_ref, nw_ref, xn_ref, project, row_split)


def _gla_inproj(h, norm_w, w_in_t, w_r3, w_gk3, b_gk, gain_row, *, key_dim, value_dim, heads, rank,
                tm=1024, steps=4):
    m, d = h.shape
    tk, tv = key_dim // steps, value_dim // steps
    kern = functools.partial(_gla_inproj_kernel, q_scale=float((key_dim // heads) ** -0.5),
                             rank=rank, row_split=1)
    declared = (2 * _nbytes((tm, d), F32) + _nbytes((tm, d), BF16)
                + 4 * _nbytes((d, tk), BF16) + 4 * _nbytes((d, tv), BF16)
                + 2 * _nbytes((d, LANES), BF16) + _nbytes((tm, LANES), BF16)
                + 2 * (3 * _nbytes((tm, tk), BF16) + 2 * _nbytes((tm, tk), F32)
                       + _nbytes((tm, tv), F32) + _nbytes((tm, tv), BF16)))
    k_first, v_first, g_first = key_dim // tk, 2 * key_dim // tv, (2 * key_dim + value_dim) // tv
    key_out = pl.BlockSpec((tm, tk), lambda i, j: (i, j))
    val_out = pl.BlockSpec((tm, tv), lambda i, j: (i, j))
    key_bf = jax.ShapeDtypeStruct((m, key_dim), BF16)
    key_f32 = jax.ShapeDtypeStruct((m, key_dim), F32)
    return pl.pallas_call(
        kern,
        grid=(m // tm, steps),
        in_specs=[
            pl.BlockSpec((tm, d), lambda i, j: (i, 0)),
            pl.BlockSpec((1, d), lambda i, j: (0, 0)),
            pl.BlockSpec((tk, d), lambda i, j: (j, 0)),
            pl.BlockSpec((tk, d), lambda i, j: (k_first + j, 0)),
            pl.BlockSpec((tv, d), lambda i, j: (v_first + j, 0)),
            pl.BlockSpec((tv, d), lambda i, j: (g_first + j, 0)),
            pl.BlockSpec((d, LANES), lambda i, j: (0, 0)),
            pl.BlockSpec((LANES, tk), lambda i, j: (0, j)),
            pl.BlockSpec((1, tk), lambda i, j: (0, j)),
            pl.BlockSpec((1, tv), lambda i, j: (0, j)),
        ],
        out_specs=[key_out] * 5 + [val_out, val_out],
        out_shape=[key_f32] * 2 + [key_bf] * 3 + [jax.ShapeDtypeStruct((m, value_dim), BF16),
                                     jax.ShapeDtypeStruct((m, value_dim), F32)],
        scratch_shapes=[pltpu.VMEM((tm, d), BF16), pltpu.VMEM((tm, LANES), BF16)],
        compiler_params=pltpu.CompilerParams(
            dimension_semantics=("arbitrary", "arbitrary"),
            vmem_limit_bytes=_vmem_limit(declared)),
        name="gla_inproj",
    )(h, norm_w.reshape(1, d), w_in_t, w_in_t, w_in_t, w_in_t, w_r3, w_gk3, b_gk.reshape(1, key_dim), gain_row)


def _pair_levels(c):
    i = np.arange(c)[:, None]
    j = np.arange(c)[None, :]
    x = i ^ j
    lvl = np.where(x > 0, np.floor(np.log2(np.maximum(x, 1))).astype(np.int64) + 1, 0)
    return np.where(j > i, -1, lvl).astype(np.int32)


def _row_signs(c, dk):
    r = np.arange(c)[None, :, None]
    s = np.array(SMALL_LEVELS)[:, None, None]
    return np.broadcast_to(np.where(r & s, 1.0, -1.0), (len(SMALL_LEVELS), c, dk)).astype(np.float32)


def _reference_rows(b, b_ref, row, s, c):
    dk = b.shape[-1]
    if s == 1:
        return jnp.where((row & 1) != 0, pltpu.roll(b, 1, 0), b)
    if s == 2:
        lo = [jnp.broadcast_to(b_ref[pl.ds(8 * v + 1, 1), :], (8, dk)) for v in range(c // 8)]
        hi = [jnp.broadcast_to(b_ref[pl.ds(8 * v + 5, 1), :], (8, dk)) for v in range(c // 8)]
        return jnp.where((row & 4) != 0, jnp.concatenate(hi, axis=0), jnp.concatenate(lo, axis=0))
    parts = [jnp.broadcast_to(b_ref[pl.ds(m * 2 * s + s - 1, 1), :], (2 * s, dk))
             for m in range(c // (2 * s))]
    return parts[0] if len(parts) == 1 else jnp.concatenate(parts, axis=0)


def _level_exponent(b, b_ref, sgn_ref, row, s, c):
    dk = b.shape[-1]
    if s in SMALL_LEVELS:
        return (b - _reference_rows(b, b_ref, row, s, c)) * sgn_ref[SMALL_LEVELS.index(s)]
    parts = []
    for m in range(c // (2 * s)):
        lo = m * 2 * s
        r = jnp.broadcast_to(b_ref[pl.ds(lo + s - 1, 1), :], (s, dk))
        parts += [r - b[lo:lo + s], b[lo + s:lo + 2 * s] - r]
    return jnp.concatenate(parts, axis=0)


def _gla_chunk_kernel(*refs, chunk, n_chunks, group, n_cast):
    lvl_ref, sgn_ref, q_ref, k_ref, g1_ref, g2_ref, g3_ref, v_ref, gt_ref = refs[:9]
    cast_src = refs[9:9 + n_cast]
    o_ref = refs[9 + n_cast]
    cast_dst = refs[10 + n_cast:10 + 2 * n_cast]
    st_ref, b0_ref, b1_ref, op_ref = refs[10 + 2 * n_cast:]
    c = chunk

    for src, dst in zip(cast_src, cast_dst):
        dst[...] = src[...].astype(dst.dtype)
    dk = q_ref.shape[-1] // group
    dv = v_ref.shape[-1] // group

    @pl.when(pl.program_id(2) == 0)
    def _():
        st_ref[...] = jnp.zeros_like(st_ref)
        op_ref[...] = jnp.zeros_like(op_ref)

    lvl = lvl_ref[...]
    tril = jnp.where(lvl >= 0, 1.0, 0.0).astype(BF16)
    row = lax.broadcasted_iota(jnp.int32, (c, dk), 0)
    heads = range(group)
    kcols = [slice(hh * dk, (hh + 1) * dk) for hh in heads]
    vcols = [slice(hh * dv, (hh + 1) * dv) for hh in heads]

    def chunk_rows(ci):
        return pl.ds(pl.multiple_of(ci * c, c), c)

    def cumulative_decay(rows, b_ref):
        for hh in heads:
            b_ref[hh] = (_dot(tril, g1_ref[rows, kcols[hh]])
                         + (_dot(tril, g2_ref[rows, kcols[hh]]) + _dot(tril, g3_ref[rows, kcols[hh]])))

    def finish(rows):
        for hh in heads:
            o = op_ref[hh]
            ms = jnp.mean(o * o, axis=-1, keepdims=True)
            on = o * lax.rsqrt(ms + NORM_EPS) * gt_ref[rows, vcols[hh]]
            o_ref[rows, vcols[hh]] = on.astype(o_ref.dtype)

    def chunk_step(ci, b_cur_ref, b_next_ref):
        rows = chunk_rows(ci)
        cumulative_decay(chunk_rows(jnp.minimum(ci + 1, n_chunks - 1)), b_next_ref)
        finish(chunk_rows(jnp.maximum(ci - 1, 0)))

        for hh in heads:
            q = q_ref[rows, kcols[hh]]
            k = k_ref[rows, kcols[hh]]
            b = b_cur_ref[hh]
            b_last = b_cur_ref[hh, pl.ds(c - 1, 1), :]
            st = st_ref[hh]
            op_ref[hh] = _dot_nt((q * jnp.exp2(b)).astype(BF16), st.astype(BF16))
            kl = (k * jnp.exp2(b_last - b)).astype(BF16)
            st_ref[hh] = st * jnp.exp2(b_last) + _dot_tn(v_ref[rows, vcols[hh]], kl)

        scores = []
        for hh in heads:
            q = q_ref[rows, kcols[hh]].astype(BF16)
            k = k_ref[rows, kcols[hh]].astype(BF16)
            b = b_cur_ref[hh]
            sc = jnp.where(lvl == 0, _dot_nt(q, k), 0.0)
            s = 1
            level = 1
            while s < c:
                e = jnp.exp2(_level_exponent(b, b_cur_ref.at[hh], sgn_ref, row, s, c)).astype(BF16)
                p = _dot_nt(q * e, k * e)
                sc = jnp.where(lvl == level, p, sc)
                s *= 2
                level += 1
            scores.append(sc.astype(BF16))
        for hh in heads:
            op_ref[hh] += _dot(scores[hh], v_ref[rows, vcols[hh]])

    def body(pair, carry):
        chunk_step(2 * pair, b0_ref, b1_ref)
        chunk_step(2 * pair + 1, b1_ref, b0_ref)
        return carry

    cumulative_decay(chunk_rows(0), b0_ref)
    lax.fori_loop(0, n_chunks // 2, body, 0)
    finish(chunk_rows(n_chunks - 1))


def _gla_chunk(q, k, g_split, v, gate, *, batch, heads, group, rows_per_step, chunk=CHUNK,
               cast_jobs=()):
    m, kd = q.shape
    vd = v.shape[1]
    dk, dv = kd // heads, vd // heads
    gk, gv = group * dk, group * dv
    seq = m // batch
    tc = min(rows_per_step, seq)
    steps = seq // tc
    assert tc % (2 * chunk) == 0
    kern = functools.partial(_gla_chunk_kernel, chunk=chunk, n_chunks=tc // chunk, group=group,
                             n_cast=len(cast_jobs))
    lvl = jnp.asarray(_pair_levels(chunk))
    sgn = jnp.asarray(_row_signs(chunk, dk))

    def rows_map(b, h, t):
        return b * steps + t, h

    declared = (2 * (3 * _nbytes((tc, gk), BF16) + 2 * _nbytes((tc, gk), F32)
                     + 2 * _nbytes((tc, gv), BF16) + _nbytes((tc, gv), F32))
                + _nbytes((group, dv, dk), F32) + 2 * _nbytes((group, chunk, dk), F32)
                + _nbytes((group, chunk, dv), F32))
    key_spec = pl.BlockSpec((tc, gk), rows_map)
    val_spec = pl.BlockSpec((tc, gv), rows_map)

    grid = (batch, heads // group, steps)
    n_steps = batch * (heads // group) * steps

    def flat_step(b, h, t):
        return (b * (heads // group) + h) * steps + t

    cast_in_specs, cast_out_specs, cast_out_shapes = [], [], []
    for w, layer, rows in cast_jobs:
        cols = w.shape[2]
        rb = rows // n_steps
        assert rb * n_steps == rows and rb % 16 == 0, (w.shape, n_steps)
        cast_in_specs.append(pl.BlockSpec(
            (None, rb, cols), functools.partial(lambda b, h, t, layer: (layer, flat_step(b, h, t), 0),
                                                layer=layer)))
        cast_out_specs.append(pl.BlockSpec((rb, cols), lambda b, h, t: (flat_step(b, h, t), 0)))
        cast_out_shapes.append(jax.ShapeDtypeStruct((rows, cols), BF16))
        declared += 2 * (_nbytes((rb, cols), F32) + _nbytes((rb, cols), BF16))

    outs = pl.pallas_call(
        kern,
        grid=grid,
        in_specs=[pl.BlockSpec(lvl.shape, lambda b, h, t: (0, 0)),
                  pl.BlockSpec(sgn.shape, lambda b, h, t: (0, 0, 0))] + [key_spec] * 5 + [val_spec] * 2
                 + cast_in_specs,
        out_specs=[val_spec] + cast_out_specs,
        out_shape=[jax.ShapeDtypeStruct((m, vd), BF16)] + cast_out_shapes,
        scratch_shapes=[pltpu.VMEM((group, dv, dk), F32), pltpu.VMEM((group, chunk, dk), F32),
                        pltpu.VMEM((group, chunk, dk), F32), pltpu.VMEM((group, chunk, dv), F32)],
        compiler_params=pltpu.CompilerParams(
            dimension_semantics=("arbitrary", "arbitrary", "arbitrary"),
            vmem_limit_bytes=_vmem_limit(declared)),
        name="gla_chunk",
    )(lvl, sgn, q, k, *g_split, v, gate, *[job[0] for job in cast_jobs])
    return outs[0], outs[1:]


def _outproj_kernel(o_ref, w_ref, h_ref, nw_ref, out_ref):
    mixed = _dot(o_ref[...], w_ref[...])
    out_ref[...] = h_ref[...] + _rms_scale(mixed, nw_ref[...])


def _outproj(o, w_out, h, norm_w, *, tm=512):
    m, d = h.shape
    dv = o.shape[1]
    declared = (2 * _nbytes((tm, dv), BF16) + 2 * _nbytes((dv, d), BF16) + 4 * _nbytes((tm, d), F32))
    return pl.pallas_call(
        _outproj_kernel,
        grid=(m // tm,),
        in_specs=[
            pl.BlockSpec((tm, dv), lambda i: (i, 0)),
            pl.BlockSpec((dv, d), lambda i: (0, 0)),
            pl.BlockSpec((tm, d), lambda i: (i, 0)),
            pl.BlockSpec((1, d), lambda i: (0, 0)),
        ],
        out_specs=pl.BlockSpec((tm, d), lambda i: (i, 0)),
        out_shape=jax.ShapeDtypeStruct((m, d), F32),
        compiler_params=pltpu.CompilerParams(
            dimension_semantics=("arbitrary",),
            vmem_limit_bytes=_vmem_limit(declared)),
        name="outproj",
    )(o, w_out, h, norm_w.reshape(1, d))


def _mlp_kernel(h_ref, pre_ref, wu_ref, wd_ref, post_ref, out_ref, xn_ref):
    kstep = pl.program_id(1)
    last = pl.num_programs(1) - 1
    tm = out_ref.shape[0]
    halves = [slice(r, r + tm // MLP_ROW_SPLIT) for r in range(0, tm, tm // MLP_ROW_SPLIT)]

    def hidden(xn):
        u = jnp.maximum(_dot(xn, wu_ref[...]), 0.0)
        return (u * u).astype(BF16)

    def down(u2, rows, add_to):
        for n in range(0, out_ref.shape[1], MLP_DOWN_COLS):
            cols = slice(n, n + MLP_DOWN_COLS)
            part = _dot(u2, wd_ref[:, cols])
            out_ref[rows, cols] = part if add_to is None else add_to[rows, cols] + part

    @pl.when(kstep == 0)
    def _():
        for rows in halves:
            xn = _rms_scale(h_ref[rows, :], pre_ref[...]).astype(BF16)
            xn_ref[rows, :] = xn
            down(hidden(xn), rows, None)

    @pl.when((kstep > 0) & (kstep < last))
    def _():
        down(hidden(xn_ref[...]), slice(None), out_ref)

    @pl.when(kstep == last)
    def _():
        for rows in halves:
            down(hidden(xn_ref[rows, :]), rows, out_ref)
            out_ref[rows, :] = h_ref[rows, :] + _rms_scale(out_ref[rows, :], post_ref[...])


def _mlp(h, pre_w, w_up, w_down, post_w, *, tm=1024, tk=1024):
    m, d = h.shape
    dff = w_up.shape[1]
    declared = (4 * _nbytes((tm, d), F32) + 2 * _nbytes((d, tk), BF16) + 2 * _nbytes((tk, d), BF16)
                + _nbytes((tm, d), BF16) + 2 * _nbytes((tm, tk), F32))
    return pl.pallas_call(
        _mlp_kernel,
        grid=(m // tm, dff // tk),
        in_specs=[
            pl.BlockSpec((tm, d), lambda i, k: (i, 0)),
            pl.BlockSpec((1, d), lambda i, k: (0, 0)),
            pl.BlockSpec((d, tk), lambda i, k: (0, k)),
            pl.BlockSpec((tk, d), lambda i, k: (k, 0)),
            pl.BlockSpec((1, d), lambda i, k: (0, 0)),
        ],
        out_specs=pl.BlockSpec((tm, d), lambda i, k: (i, 0)),
        out_shape=jax.ShapeDtypeStruct((m, d), F32),
        scratch_shapes=[pltpu.VMEM((tm, d), BF16)],
        compiler_params=pltpu.CompilerParams(
            dimension_semantics=("arbitrary", "arbitrary"),
            vmem_limit_bytes=_vmem_limit(declared)),
        name="mlp",
    )(h, pre_w.reshape(1, d), w_up, w_down, post_w.reshape(1, d))


def kernel(x, norm_mix_pre, norm_mix_post, norm_mlp_pre, norm_mlp_post, hgrn_w_in, hgrn_lb_logits, hgrn_norm, hgrn_w_out, gla_w_in, gla_w_gk, gla_b_gk, gla_norm, gla_w_out, mlp_w_up, mlp_w_down):
    batch, seq, d = x.shape
    depth = norm_mix_pre.shape[0]
    h = x.reshape(batch * seq, d)
    later = {}
    for layer in range(depth):
        j = layer // 2
        if layer % 2 == 0:
            if layer > 0:
                later["hgrn_in", j] = hgrn_w_in
            later["hgrn_out", j] = hgrn_w_out
        else:
            later["gla_in", j] = jnp.swapaxes(gla_w_in, 1, 2)
            later["gla_out", j] = gla_w_out
        later["mlp_up", layer] = mlp_w_up
        later["mlp_down", layer] = mlp_w_down
    bf16_w = {("hgrn_in", 0): hgrn_w_in[0].astype(BF16)}
    gla_rank = gla_w_gk.shape[1]
    gla_rows = gla_w_in.shape[2] - gla_rank

    for layer in range(depth):
        j = layer // 2
        cast_jobs = [(w, key[1], gla_rows if key[0] == "gla_in" else w.shape[1])
                     for key, w in later.items()] if layer == 0 else []
        if layer % 2 == 0:
            gain_row = jnp.tile(hgrn_norm[j], HGRN_HEADS).reshape(1, -1)
            q, k, g1, g2, g3, v, gate = _hgrn_inproj(h, norm_mix_pre[layer], bf16_w["hgrn_in", j],
                                                     hgrn_lb_logits, gain_row, layer)
            o, cast = _gla_chunk(q, k, (g1, g2, g3), v, gate, batch=batch, heads=HGRN_HEADS,
                                 group=16, rows_per_step=256, cast_jobs=cast_jobs)
            w_out = ("hgrn_out", j)
        else:
            key_dim = gla_w_gk.shape[2]
            rank = gla_w_gk.shape[1]
            value_dim = (gla_w_in.shape[2] - 2 * key_dim - rank) // 2
            n_main = 2 * key_dim + 2 * value_dim
            assert 3 * rank <= LANES
            w_r3 = jnp.pad(jnp.tile(gla_w_in[j, :, n_main:], (1, 3)),
                           ((0, 0), (0, LANES - 3 * rank))).astype(BF16)
            gk_hi = gla_w_gk[j].astype(BF16)
            gk_lo = (gla_w_gk[j] - gk_hi.astype(F32)).astype(BF16)
            w_gk3 = jnp.pad(jnp.concatenate([gk_hi, gk_hi, gk_lo], axis=0), ((0, LANES - 3 * rank), (0, 0)))
            gain_row = jnp.tile(gla_norm[j], GLA_HEADS).reshape(1, -1)
            q, k, g1, g2, g3, v, gate = _gla_inproj(h, norm_mix_pre[layer], bf16_w["gla_in", j], w_r3,
                                                    w_gk3, gla_b_gk[j], gain_row, key_dim=key_dim,
                                                    value_dim=value_dim, heads=GLA_HEADS, rank=rank)
            o, cast = _gla_chunk(q, k, (g1, g2, g3), v, gate, batch=batch, heads=GLA_HEADS,
                                 group=4, rows_per_step=512, chunk=128, cast_jobs=cast_jobs)
            w_out = ("gla_out", j)
        if layer == 0:
            bf16_w.update(zip(later.keys(), cast))
        h = _outproj(o, bf16_w[w_out], h, norm_mix_post[layer])
        h = _mlp(h, norm_mlp_pre[layer], bf16_w["mlp_up", layer], bf16_w["mlp_down", layer],
                 norm_mlp_post[layer])
    return h.reshape(batch, seq, d)
```

```python
import functools
import math

import numpy as np
import jax
import jax.numpy as jnp
from jax import lax
from jax.experimental import pallas as pl
from jax.experimental.pallas import tpu as pltpu

F32 = jnp.float32
BF16 = jnp.bfloat16

NORM_EPS = 1e-6
CHUNK = 64
HGRN_HEADS = 16
GLA_HEADS = 4
GLA_GATE_NORMALIZER = 16.0
LOG2_E = math.log2(math.e)

V7X_VMEM_BYTES = 64 * 1024 * 1024
VMEM_RESERVED_BYTES = 2 * 1024 * 1024
VMEM_HEADROOM_BYTES = 6 * 1024 * 1024
LANES = 128
SUBLANES = 8
SMALL_LEVELS = tuple(2 ** i for i in range(int(math.log2(SUBLANES))))
MLP_DOWN_COLS = 512
MLP_ROW_SPLIT = 2


def _vmem_limit(declared_bytes):
    return int(min(V7X_VMEM_BYTES - VMEM_RESERVED_BYTES, declared_bytes + VMEM_HEADROOM_BYTES))


def _nbytes(shape, dtype):
    return int(np.prod(shape)) * jnp.dtype(dtype).itemsize


def _dot(a, b):
    return jnp.dot(a, b, preferred_element_type=F32)


def _dot_nt(a, b):
    return lax.dot_general(a, b, (((1,), (1,)), ((), ())), preferred_element_type=F32)


def _dot_tn(a, b):
    return lax.dot_general(a, b, (((0,), (0,)), ((), ())), preferred_element_type=F32)


def _sigmoid(x):
    return 0.5 * jnp.tanh(0.5 * x) + 0.5


def _silu(x):
    return x * _sigmoid(x)


def _rms_scale(x, w):
    ms = jnp.mean(x * x, axis=-1, keepdims=True)
    return x * lax.rsqrt(ms + NORM_EPS) * w


def _split3_bf16(x):
    a = x.astype(BF16)
    r = x - a.astype(F32)
    b = r.astype(BF16)
    return a, b, (r - b.astype(F32)).astype(BF16)


def _project_row_tile(x_ref, nw_ref, xn_ref, project, row_split):
    first = pl.program_id(1) == 0
    tm = x_ref.shape[0]

    @pl.when(first)
    def _():
        for r in range(0, tm, tm // row_split):
            rows = slice(r, r + tm // row_split)
            xn = _rms_scale(x_ref[rows, :], nw_ref[...]).astype(BF16)
            xn_ref[rows, :] = xn
            project(xn, rows, True)

    @pl.when(jnp.logical_not(first))
    def _():
        project(xn_ref[...], slice(None), False)


def _hgrn_inproj_kernel(x_ref, nw_ref, wq_ref, wf_ref, wi_ref, wg_ref, lbl_ref, gain_ref,
                        q_ref, k_ref, g_ref, v_ref, gt_ref, xn_ref, *, layer, row_split):
    def project(xn, rows, first_step):
        lg = lbl_ref[...]
        e = jnp.exp(lg - jnp.max(lg, axis=0, keepdims=True))
        lb = jnp.sum(e[:layer + 1], axis=0, keepdims=True) / jnp.sum(e, axis=0, keepdims=True)
        forget = lb + (1.0 - lb) * _sigmoid(_dot(xn, wf_ref[...]))
        k_ref[rows, :] = 1.0 - forget
        g_ref[0, rows, :], g_ref[1, rows, :], g_ref[2, rows, :] = _split3_bf16(jnp.log2(forget))
        gt_ref[rows, :] = _silu(_dot(xn, wg_ref[...])) * gain_ref[...]
        q_ref[rows, :] = _silu(_dot(xn, wq_ref[...]))
        v_ref[rows, :] = _dot(xn, wi_ref[...]).astype(BF16)

    _project_row_tile(x_ref, nw_ref, xn_ref, project, row_split)


def _hgrn_inproj(h, norm_w, w_in, lb_logits, gain_row, layer, *, tm=1024, tn=512):
    m, d = h.shape
    df = w_in.shape[1] // 4
    tiles = df // tn
    kern = functools.partial(_hgrn_inproj_kernel, layer=layer, row_split=2)
    bf_out = jax.ShapeDtypeStruct((m, df), BF16)
    f32_out = jax.ShapeDtypeStruct((m, df), F32)
    declared = (2 * _nbytes((tm, d), F32) + _nbytes((tm, d), BF16) + 8 * _nbytes((d, tn), BF16)
                + 2 * (4 * _nbytes((tm, tn), BF16) + 3 * _nbytes((tm, tn), F32)))
    w_spec =[pl.BlockSpec((d, tn), functools.partial(lambda i, j, g: (0, g * tiles + j), g=g))
              for g in range(4)]
    out_spec = pl.BlockSpec((tm, tn), lambda i, j: (i, j))
    return pl.pallas_call(
        kern,
        grid=(m // tm, tiles),
        in_specs=[
            pl.BlockSpec((tm, d), lambda i, j: (i, 0)),
            pl.BlockSpec((1, d), lambda i, j: (0, 0)),
            *w_spec,
            pl.BlockSpec((lb_logits.shape[0], tn), lambda i, j: (0, j)),
            pl.BlockSpec((1, tn), lambda i, j: (0, j)),
        ],
        out_specs=[out_spec, out_spec, pl.BlockSpec((3, tm, tn), lambda i, j: (0, i, j)), out_spec, out_spec],
        out_shape=[f32_out, f32_out, jax.ShapeDtypeStruct((3, m, df), BF16), bf_out, f32_out],
        scratch_shapes=[pltpu.VMEM((tm, d), BF16)],
        compiler_params=pltpu.CompilerParams(
            dimension_semantics=("arbitrary", "arbitrary"),
            vmem_limit_bytes=_vmem_limit(declared)),
        name="hgrn_inproj",
    )(h, norm_w.reshape(1, d), w_in, w_in, w_in, w_in, lb_logits, gain_row)


def _gla_inproj_kernel(x_ref, nw_ref, wq_ref, wk_ref, wv_ref, wg_ref, wr_ref, wgk_ref, bgk_ref,
                       gain_ref, q_ref, k_ref, g_ref, v_ref, gt_ref,
                       xn_ref, r_ref, *, q_scale, rank, row_split):
    def project(xn, rows, first_step):
        if first_step:
            r = _dot(xn, wr_ref[...])
            head = r.astype(BF16).astype(F32)
            lane = lax.broadcasted_iota(jnp.int32, r.shape, 1)
            middle = (lane >= rank) & (lane < 2 * rank)
            r_ref[rows, :] = jnp.where(middle, r - head, head).astype(BF16)
        z = _dot(r_ref[rows, :], wgk_ref[...]) + bgk_ref[...]
        log_sig = jnp.minimum(z, 0.0) - jnp.log(1.0 + jnp.exp(-jnp.abs(z)))
        g_ref[0, rows, :], g_ref[1, rows, :], g_ref[2, rows, :] = _split3_bf16(
            log_sig * (LOG2_E / GLA_GATE_NORMALIZER))
        gt_ref[rows, :] = _silu(_dot_nt(xn, wg_ref[...])) * gain_ref[...]
        q_ref[rows, :] = _dot_nt(xn, wq_ref[...]) * q_scale
        k_ref[rows, :] = _dot_nt(xn, wk_ref[...])
        v_ref[rows, :] = _dot_nt(xn, wv_ref[...]).astype(BF16)

    _project_row_tile(x_ref, nw_ref, xn_ref, project, row_split)


def _gla_inproj(h, norm_w, w_in_t, w_r3, w_gk3, b_gk, gain_row, *, key_dim, value_dim, heads, rank,
                tm=1024, steps=4):
    m, d = h.shape
    tk, tv = key_dim // steps, value_dim // steps
    kern = functools.partial(_gla_inproj_kernel, q_scale=float((key_dim // heads) ** -0.5),
                             rank=rank, row_split=1)
    declared = (2 * _nbytes((tm, d), F32) + _nbytes((tm, d), BF16)
                + 4 * _nbytes((d, tk), BF16) + 4 * _nbytes((d, tv), BF16)
                + 2 * _nbytes((d, LANES), BF16) + _nbytes((tm, LANES), BF16)
                + 2 * (3 * _nbytes((tm, tk), BF16) + 2 * _nbytes((tm, tk), F32)
                       + _nbytes((tm, tv), F32) + _nbytes((tm, tv), BF16)))
    k_first, v_first, g_first = key_dim // tk, 2 * key_dim // tv, (2 * key_dim + value_dim) // tv
    key_out = pl.BlockSpec((tm, tk), lambda i, j: (i, j))
    val_out = pl.BlockSpec((tm, tv), lambda i, j: (i, j))
    key_f32 = jax.ShapeDtypeStruct((m, key_dim), F32)
    return pl.pallas_call(
        kern,
        grid=(m // tm, steps),
        in_specs=[
            pl.BlockSpec((tm, d), lambda i, j: (i, 0)),
            pl.BlockSpec((1, d), lambda i, j: (0, 0)),
            pl.BlockSpec((tk, d), lambda i, j: (j, 0)),
            pl.BlockSpec((tk, d), lambda i, j: (k_first + j, 0)),
            pl.BlockSpec((tv, d), lambda i, j: (v_first + j, 0)),
            pl.BlockSpec((tv, d), lambda i, j: (g_first + j, 0)),
            pl.BlockSpec((d, LANES), lambda i, j: (0, 0)),
            pl.BlockSpec((LANES, tk), lambda i, j: (0, j)),
            pl.BlockSpec((1, tk), lambda i, j: (0, j)),
            pl.BlockSpec((1, tv), lambda i, j: (0, j)),
        ],
        out_specs=[key_out, key_out, pl.BlockSpec((3, tm, tk), lambda i, j: (0, i, j)), val_out, val_out],
        out_shape=[key_f32, key_f32, jax.ShapeDtypeStruct((3, m, key_dim), BF16)] + [jax.ShapeDtypeStruct((m, value_dim), BF16),
                                     jax.ShapeDtypeStruct((m, value_dim), F32)],
        scratch_shapes=[pltpu.VMEM((tm, d), BF16), pltpu.VMEM((tm, LANES), BF16)],
        compiler_params=pltpu.CompilerParams(
            dimension_semantics=("arbitrary", "arbitrary"),
            vmem_limit_bytes=_vmem_limit(declared)),
        name="gla_inproj",
    )(h, norm_w.reshape(1, d), w_in_t, w_in_t, w_in_t, w_in_t, w_r3, w_gk3, b_gk.reshape(1, key_dim), gain_row)


def _pair_levels(c):
    i = np.arange(c)[:, None]
    j = np.arange(c)[None, :]
    x = i ^ j
    lvl = np.where(x > 0, np.floor(np.log2(np.maximum(x, 1))).astype(np.int64) + 1, 0)
    return np.where(j > i, -1, lvl).astype(np.int32)


def _row_signs(c, dk):
    r = np.arange(c)[None, :, None]
    s = np.array(SMALL_LEVELS)[:, None, None]
    return np.broadcast_to(np.where(r & s, 1.0, -1.0), (len(SMALL_LEVELS), c, dk)).astype(np.float32)


def _reference_rows(b, b_ref, row, s, c):
    dk = b.shape[-1]
    if s == 1:
        return jnp.where((row & 1) != 0, pltpu.roll(b, 1, 0), b)
    if s == 2:
        lo = [jnp.broadcast_to(b_ref[pl.ds(8 * v + 1, 1), :], (8, dk)) for v in range(c // 8)]
        hi = [jnp.broadcast_to(b_ref[pl.ds(8 * v + 5, 1), :], (8, dk)) for v in range(c // 8)]
        return jnp.where((row & 4) != 0, jnp.concatenate(hi, axis=0), jnp.concatenate(lo, axis=0))
    parts = [jnp.broadcast_to(b_ref[pl.ds(m * 2 * s + s - 1, 1), :], (2 * s, dk))
             for m in range(c // (2 * s))]
    return parts[0] if len(parts) == 1 else jnp.concatenate(parts, axis=0)


def _level_exponent(b, b_ref, sgn_ref, row, s, c):
    dk = b.shape[-1]
    if s in SMALL_LEVELS:
        return (b - _reference_rows(b, b_ref, row, s, c)) * sgn_ref[SMALL_LEVELS.index(s)]
    parts = []
    for m in range(c // (2 * s)):
        lo = m * 2 * s
        r = jnp.broadcast_to(b_ref[pl.ds(lo + s - 1, 1), :], (s, dk))
        parts += [r - b[lo:lo + s], b[lo + s:lo + 2 * s] - r]
    return jnp.concatenate(parts, axis=0)


def _gla_chunk_kernel(*refs, chunk, n_chunks, group, n_cast):
    lvl_ref, sgn_ref, q_ref, k_ref, g_ref, v_ref, gt_ref = refs[:7]
    cast_src = refs[7:7 + n_cast]
    o_ref = refs[7 + n_cast]
    cast_dst = refs[8 + n_cast:8 + 2 * n_cast]
    st_ref, b0_ref, b1_ref, op_ref = refs[8 + 2 * n_cast:]
    c = chunk

    for src, dst in zip(cast_src, cast_dst):
        dst[...] = src[...].astype(dst.dtype)
    dk = q_ref.shape[-1] // group
    dv = v_ref.shape[-1] // group

    @pl.when(pl.program_id(2) == 0)
    def _():
        st_ref[...] = jnp.zeros_like(st_ref)
        op_ref[...] = jnp.zeros_like(op_ref)

    lvl = lvl_ref[...]
    tril = jnp.where(lvl >= 0, 1.0, 0.0).astype(BF16)
    row = lax.broadcasted_iota(jnp.int32, (c, dk), 0)
    heads = range(group)
    kcols = [slice(hh * dk, (hh + 1) * dk) for hh in heads]
    vcols = [slice(hh * dv, (hh + 1) * dv) for hh in heads]

    def chunk_rows(ci):
        return pl.ds(pl.multiple_of(ci * c, c), c)

    def cumulative_decay(rows, b_ref):
        for hh in heads:
            b_ref[hh] = (_dot(tril, g_ref[0, rows, kcols[hh]])
                         + (_dot(tril, g_ref[1, rows, kcols[hh]]) + _dot(tril, g_ref[2, rows, kcols[hh]])))

    def finish(rows):
        for hh in heads:
            o = op_ref[hh]
            ms = jnp.mean(o * o, axis=-1, keepdims=True)
            on = o * lax.rsqrt(ms + NORM_EPS) * gt_ref[rows, vcols[hh]]
            o_ref[rows, vcols[hh]] = on.astype(o_ref.dtype)

    def chunk_step(ci, b_cur_ref, b_next_ref):
        rows = chunk_rows(ci)
        cumulative_decay(chunk_rows(jnp.minimum(ci + 1, n_chunks - 1)), b_next_ref)
        finish(chunk_rows(jnp.maximum(ci - 1, 0)))

        for hh in heads:
            q = q_ref[rows, kcols[hh]]
            k = k_ref[rows, kcols[hh]]
            b = b_cur_ref[hh]
            b_last = b_cur_ref[hh, pl.ds(c - 1, 1), :]
            st = st_ref[hh]
            op_ref[hh] = _dot_nt((q * jnp.exp2(b)).astype(BF16), st.astype(BF16))
            kl = (k * jnp.exp2(b_last - b)).astype(BF16)
            st_ref[hh] = st * jnp.exp2(b_last) + _dot_tn(v_ref[rows, vcols[hh]], kl)

        scores = []
        for hh in heads:
            q = q_ref[rows, kcols[hh]].astype(BF16)
            k = k_ref[rows, kcols[hh]].astype(BF16)
            b = b_cur_ref[hh]
            sc = jnp.where(lvl == 0, _dot_nt(q, k), 0.0)
            s = 1
            level = 1
            while s < c:
                e = jnp.exp2(_level_exponent(b, b_cur_ref.at[hh], sgn_ref, row, s, c)).astype(BF16)
                p = _dot_nt(q * e, k * e)
                sc = jnp.where(lvl == level, p, sc)
                s *= 2
                level += 1
            scores.append(sc.astype(BF16))
        for hh in heads:
            op_ref[hh] += _dot(scores[hh], v_ref[rows, vcols[hh]])

    def body(pair, carry):
        chunk_step(2 * pair, b0_ref, b1_ref)
        chunk_step(2 * pair + 1, b1_ref, b0_ref)
        return carry

    cumulative_decay(chunk_rows(0), b0_ref)
    lax.fori_loop(0, n_chunks // 2, body, 0)
    finish(chunk_rows(n_chunks - 1))


def _gla_chunk(q, k, g_split, v, gate, *, batch, heads, group, rows_per_step, chunk=CHUNK,
               cast_jobs=()):
    m, kd = q.shape
    vd = v.shape[1]
    dk, dv = kd // heads, vd // heads
    gk, gv = group * dk, group * dv
    seq = m // batch
    tc = min(rows_per_step, seq)
    steps = seq // tc
    assert tc % (2 * chunk) == 0
    kern = functools.partial(_gla_chunk_kernel, chunk=chunk, n_chunks=tc // chunk, group=group,
                             n_cast=len(cast_jobs))
    lvl = jnp.asarray(_pair_levels(chunk))
    sgn = jnp.asarray(_row_signs(chunk, dk))

    def rows_map(b, h, t):
        return b * steps + t, h

    declared = (2 * (3 * _nbytes((tc, gk), BF16) + 2 * _nbytes((tc, gk), F32)
                     + 2 * _nbytes((tc, gv), BF16) + _nbytes((tc, gv), F32))
                + _nbytes((group, dv, dk), F32) + 2 * _nbytes((group, chunk, dk), F32)
                + _nbytes((group, chunk, dv), F32))
    key_spec = pl.BlockSpec((tc, gk), rows_map)
    val_spec = pl.BlockSpec((tc, gv), rows_map)

    grid = (batch, heads // group, steps)
    n_steps = batch * (heads // group) * steps

    def flat_step(b, h, t):
        return (b * (heads // group) + h) * steps + t

    cast_in_specs, cast_out_specs, cast_out_shapes = [], [], []
    for w, layer, rows in cast_jobs:
        cols = w.shape[2]
        rb = rows // n_steps
        assert rb * n_steps == rows and rb % 16 == 0, (w.shape, n_steps)
        cast_in_specs.append(pl.BlockSpec(
            (None, rb, cols), functools.partial(lambda b, h, t, layer: (layer, flat_step(b, h, t), 0),
                                                layer=layer)))
        cast_out_specs.append(pl.BlockSpec((rb, cols), lambda b, h, t: (flat_step(b, h, t), 0)))
        cast_out_shapes.append(jax.ShapeDtypeStruct((rows, cols), BF16))
        declared += 2 * (_nbytes((rb, cols), F32) + _nbytes((rb, cols), BF16))

    outs = pl.pallas_call(
        kern,
        grid=grid,
        in_specs=[pl.BlockSpec(lvl.shape, lambda b, h, t: (0, 0)),
                  pl.BlockSpec(sgn.shape, lambda b, h, t: (0, 0, 0)), key_spec, key_spec,
                  pl.BlockSpec((3, tc, gk), lambda b, h, t: (0, b * steps + t, h)), val_spec, val_spec]
                 + cast_in_specs,
        out_specs=[val_spec] + cast_out_specs,
        out_shape=[jax.ShapeDtypeStruct((m, vd), BF16)] + cast_out_shapes,
        scratch_shapes=[pltpu.VMEM((group, dv, dk), F32), pltpu.VMEM((group, chunk, dk), F32),
                        pltpu.VMEM((group, chunk, dk), F32), pltpu.VMEM((group, chunk, dv), F32)],
        compiler_params=pltpu.CompilerParams(
            dimension_semantics=("arbitrary", "arbitrary", "arbitrary"),
            vmem_limit_bytes=_vmem_limit(declared)),
        name="gla_chunk",
    )(lvl, sgn, q, k, g_split, v, gate, *[job[0] for job in cast_jobs])
    return outs[0], outs[1:]


def _outproj_kernel(o_ref, w_ref, h_ref, nw_ref, out_ref):
    mixed = _dot(o_ref[...], w_ref[...])
    out_ref[...] = h_ref[...] + _rms_scale(mixed, nw_ref[...])


def _outproj(o, w_out, h, norm_w, *, tm=512):
    m, d = h.shape
    dv = o.shape[1]
    declared = (2 * _nbytes((tm, dv), BF16) + 2 * _nbytes((dv, d), BF16) + 4 * _nbytes((tm, d), F32))
    return pl.pallas_call(
        _outproj_kernel,
        grid=(m // tm,),
        in_specs=[
            pl.BlockSpec((tm, dv), lambda i: (i, 0)),
            pl.BlockSpec((dv, d), lambda i: (0, 0)),
            pl.BlockSpec((tm, d), lambda i: (i, 0)),
            pl.BlockSpec((1, d), lambda i: (0, 0)),
        ],
        out_specs=pl.BlockSpec((tm, d), lambda i: (i, 0)),
        out_shape=jax.ShapeDtypeStruct((m, d), F32),
        compiler_params=pltpu.CompilerParams(
            dimension_semantics=("arbitrary",),
            vmem_limit_bytes=_vmem_limit(declared)),
        name="outproj",
    )(o, w_out, h, norm_w.reshape(1, d))


def _mlp_kernel(h_ref, pre_ref, wu_ref, wd_ref, post_ref, out_ref, xn_ref):
    kstep = pl.program_id(1)
    last = pl.num_programs(1) - 1
    tm = out_ref.shape[0]
    halves = [slice(r, r + tm // MLP_ROW_SPLIT) for r in range(0, tm, tm // MLP_ROW_SPLIT)]

    def hidden(xn):
        u = jnp.maximum(_dot(xn, wu_ref[...]), 0.0)
        return (u * u).astype(BF16)

    def down(u2, rows, add_to):
        for n in range(0, out_ref.shape[1], MLP_DOWN_COLS):
            cols = slice(n, n + MLP_DOWN_COLS)
            part = _dot(u2, wd_ref[:, cols])
            out_ref[rows, cols] = part if add_to is None else add_to[rows, cols] + part

    @pl.when(kstep == 0)
    def _():
        for rows in halves:
            xn = _rms_scale(h_ref[rows, :], pre_ref[...]).astype(BF16)
            xn_ref[rows, :] = xn
            down(hidden(xn), rows, None)

    @pl.when((kstep > 0) & (kstep < last))
    def _():
        down(hidden(xn_ref[...]), slice(None), out_ref)

    @pl.when(kstep == last)
    def _():
        for rows in halves:
            down(hidden(xn_ref[rows, :]), rows, out_ref)
            out_ref[rows, :] = h_ref[rows, :] + _rms_scale(out_ref[rows, :], post_ref[...])


def _mlp(h, pre_w, w_up, w_down, post_w, *, tm=1024, tk=1024):
    m, d = h.shape
    dff = w_up.shape[1]
    declared = (4 * _nbytes((tm, d), F32) + 2 * _nbytes((d, tk), BF16) + 2 * _nbytes((tk, d), BF16)
                + _nbytes((tm, d), BF16) + 2 * _nbytes((tm, tk), F32))
    return pl.pallas_call(
        _mlp_kernel,
        grid=(m // tm, dff // tk),
        in_specs=[
            pl.BlockSpec((tm, d), lambda i, k: (i, 0)),
            pl.BlockSpec((1, d), lambda i, k: (0, 0)),
            pl.BlockSpec((d, tk), lambda i, k: (0, k)),
            pl.BlockSpec((tk, d), lambda i, k: (k, 0)),
            pl.BlockSpec((1, d), lambda i, k: (0, 0)),
        ],
        out_specs=pl.BlockSpec((tm, d), lambda i, k: (i, 0)),
        out_shape=jax.ShapeDtypeStruct((m, d), F32),
        scratch_shapes=[pltpu.VMEM((tm, d), BF16)],
        compiler_params=pltpu.CompilerParams(
            dimension_semantics=("arbitrary", "arbitrary"),
            vmem_limit_bytes=_vmem_limit(declared)),
        name="mlp",
    )(h, pre_w.reshape(1, d), w_up, w_down, post_w.reshape(1, d))


def kernel(x, norm_mix_pre, norm_mix_post, norm_mlp_pre, norm_mlp_post, hgrn_w_in, hgrn_lb_logits, hgrn_norm, hgrn_w_out, gla_w_in, gla_w_gk, gla_b_gk, gla_norm, gla_w_out, mlp_w_up, mlp_w_down):
    batch, seq, d = x.shape
    depth = norm_mix_pre.shape[0]
    h = x.reshape(batch * seq, d)
    later = {}
    for layer in range(depth):
        j = layer // 2
        if layer % 2 == 0:
            if layer > 0:
                later["hgrn_in", j] = hgrn_w_in
            later["hgrn_out", j] = hgrn_w_out
        else:
            later["gla_in", j] = jnp.swapaxes(gla_w_in, 1, 2)
            later["gla_out", j] = gla_w_out
        later["mlp_up", layer] = mlp_w_up
        later["mlp_down", layer] = mlp_w_down
    bf16_w = {("hgrn_in", 0): hgrn_w_in[0].astype(BF16)}
    gla_rank = gla_w_gk.shape[1]
    gla_rows = gla_w_in.shape[2] - gla_rank

    for layer in range(depth):
        j = layer // 2
        cast_jobs = [(w, key[1], gla_rows if key[0] == "gla_in" else w.shape[1])
                     for key, w in later.items()] if layer == 0 else []
        if layer % 2 == 0:
            gain_row = jnp.tile(hgrn_norm[j], HGRN_HEADS).reshape(1, -1)
            q, k, g, v, gate = _hgrn_inproj(h, norm_mix_pre[layer], bf16_w["hgrn_in", j],
                                                     hgrn_lb_logits, gain_row, layer)
            o, cast = _gla_chunk(q, k, g, v, gate, batch=batch, heads=HGRN_HEADS,
                                 group=16, rows_per_step=256, cast_jobs=cast_jobs)
            w_out = ("hgrn_out", j)
        else:
            key_dim = gla_w_gk.shape[2]
            rank = gla_w_gk.shape[1]
            value_dim = (gla_w_in.shape[2] - 2 * key_dim - rank) // 2
            n_main = 2 * key_dim + 2 * value_dim
            assert 3 * rank <= LANES
            w_r3 = jnp.pad(jnp.tile(gla_w_in[j, :, n_main:], (1, 3)),
                           ((0, 0), (0, LANES - 3 * rank))).astype(BF16)
            gk_hi = gla_w_gk[j].astype(BF16)
            gk_lo = (gla_w_gk[j] - gk_hi.astype(F32)).astype(BF16)
            w_gk3 = jnp.pad(jnp.concatenate([gk_hi, gk_hi, gk_lo], axis=0), ((0, LANES - 3 * rank), (0, 0)))
            gain_row = jnp.tile(gla_norm[j], GLA_HEADS).reshape(1, -1)
            q, k, g, v, gate = _gla_inproj(h, norm_mix_pre[layer], bf16_w["gla_in", j], w_r3,
                                                    w_gk3, gla_b_gk[j], gain_row, key_dim=key_dim,
                                                    value_dim=value_dim, heads=GLA_HEADS, rank=rank)
            o, cast = _gla_chunk(q, k, g, v, gate, batch=batch, heads=GLA_HEADS,
                                 group=4, rows_per_step=512, chunk=128, cast_jobs=cast_jobs)
            w_out = ("gla_out", j)
        if layer == 0:
            bf16_w.update(zip(later.keys(), cast))
        h = _outproj(o, bf16_w[w_out], h, norm_mix_post[layer])
        h = _mlp(h, norm_mlp_pre[layer], bf16_w["mlp_up", layer], bf16_w["mlp_down", layer],
                 norm_mlp_post[layer])
    return h.reshape(batch, seq, d)
```

```python
import functools
import math

import numpy as np
import jax
import jax.numpy as jnp
from jax import lax
from jax.experimental import pallas as pl
from jax.experimental.pallas import tpu as pltpu

F32 = jnp.float32
BF16 = jnp.bfloat16

NORM_EPS = 1e-6
CHUNK = 64
HGRN_HEADS = 16
GLA_HEADS = 4
GLA_GATE_NORMALIZER = 16.0
LOG2_E = math.log2(math.e)

V7X_VMEM_BYTES = 64 * 1024 * 1024
VMEM_RESERVED_BYTES = 2 * 1024 * 1024
VMEM_HEADROOM_BYTES = 6 * 1024 * 1024
LANES = 128
SUBLANES = 8
SMALL_LEVELS = tuple(2 ** i for i in range(int(math.log2(SUBLANES))))
MLP_DOWN_COLS = 512
MLP_ROW_SPLIT = 2


def _vmem_limit(declared_bytes):
    return int(min(V7X_VMEM_BYTES - VMEM_RESERVED_BYTES, declared_bytes + VMEM_HEADROOM_BYTES))


def _nbytes(shape, dtype):
    return int(np.prod(shape)) * jnp.dtype(dtype).itemsize


def _dot(a, b):
    return jnp.dot(a, b, preferred_element_type=F32)


def _dot_nt(a, b):
    return lax.dot_general(a, b, (((1,), (1,)), ((), ())), preferred_element_type=F32)


def _dot_tn(a, b):
    return lax.dot_general(a, b, (((0,), (0,)), ((), ())), preferred_element_type=F32)


def _sigmoid(x):
    return 0.5 * jnp.tanh(0.5 * x) + 0.5


def _silu(x):
    return x * _sigmoid(x)


def _rms_scale(x, w):
    ms = jnp.mean(x * x, axis=-1, keepdims=True)
    return x * lax.rsqrt(ms + NORM_EPS) * w


def _split3_bf16(x):
    a = x.astype(BF16)
    r = x - a.astype(F32)
    b = r.astype(BF16)
    return a, b, (r - b.astype(F32)).astype(BF16)


def _project_row_tile(x_ref, nw_ref, xn_ref, project, row_split):
    first = pl.program_id(1) == 0
    tm = x_ref.shape[0]

    @pl.when(first)
    def _():
        for r in range(0, tm, tm // row_split):
            rows = slice(r, r + tm // row_split)
            xn = _rms_scale(x_ref[rows, :], nw_ref[...]).astype(BF16)
            xn_ref[rows, :] = xn
            project(xn, rows, True)

    @pl.when(jnp.logical_not(first))
    def _():
        project(xn_ref[...], slice(None), False)


def _hgrn_inproj_kernel(x_ref, nw_ref, wq_ref, wf_ref, wi_ref, wg_ref, lbl_ref, gain_ref,
                        q_ref, k_ref, g_ref, v_ref, gt_ref, xn_ref, *, layer, row_split):
    def project(xn, rows, first_step):
        lg = lbl_ref[...]
        e = jnp.exp(lg - jnp.max(lg, axis=0, keepdims=True))
        lb = jnp.sum(e[:layer + 1], axis=0, keepdims=True) / jnp.sum(e, axis=0, keepdims=True)
        forget = lb + (1.0 - lb) * _sigmoid(_dot(xn, wf_ref[...]))
        k_ref[rows, :] = 1.0 - forget
        g_ref[0, rows, :], g_ref[1, rows, :], g_ref[2, rows, :] = _split3_bf16(jnp.log2(forget))
        gt_ref[rows, :] = _silu(_dot(xn, wg_ref[...])) * gain_ref[...]
        q_ref[rows, :] = _silu(_dot(xn, wq_ref[...]))
        v_ref[rows, :] = _dot(xn, wi_ref[...]).astype(BF16)

    _project_row_tile(x_ref, nw_ref, xn_ref, project, row_split)


def _hgrn_inproj(h, norm_w, w_in, lb_logits, gain_row, layer, *, tm=1024, tn=512):
    m, d = h.shape
    df = w_in.shape[1] // 4
    tiles = df // tn
    kern = functools.partial(_hgrn_inproj_kernel, layer=layer, row_split=2)
    bf_out = jax.ShapeDtypeStruct((m, df), BF16)
    f32_out = jax.ShapeDtypeStruct((m, df), F32)
    declared = (2 * _nbytes((tm, d), F32) + _nbytes((tm, d), BF16) + 8 * _nbytes((d, tn), BF16)
                + 2 * (4 * _nbytes((tm, tn), BF16) + 3 * _nbytes((tm, tn), F32)))
    w_spec =[pl.BlockSpec((d, tn), functools.partial(lambda i, j, g: (0, g * tiles + j), g=g))
              for g in range(4)]
    out_spec = pl.BlockSpec((tm, tn), lambda i, j: (i, j))
    return pl.pallas_call(
        kern,
        grid=(m // tm, tiles),
        in_specs=[
            pl.BlockSpec((tm, d), lambda i, j: (i, 0)),
            pl.BlockSpec((1, d), lambda i, j: (0, 0)),
            *w_spec,
            pl.BlockSpec((lb_logits.shape[0], tn), lambda i, j: (0, j)),
            pl.BlockSpec((1, tn), lambda i, j: (0, j)),
        ],
        out_specs=[out_spec, out_spec, pl.BlockSpec((3, tm, tn), lambda i, j: (0, i, j)), out_spec, out_spec],
        out_shape=[f32_out, f32_out, jax.ShapeDtypeStruct((3, m, df), BF16), bf_out, f32_out],
        scratch_shapes=[pltpu.VMEM((tm, d), BF16)],
        compiler_params=pltpu.CompilerParams(
            dimension_semantics=("arbitrary", "arbitrary"),
            vmem_limit_bytes=_vmem_limit(declared)),
        name="hgrn_inproj",
    )(h, norm_w.reshape(1, d), w_in, w_in, w_in, w_in, lb_logits, gain_row)


def _gla_inproj_kernel(x_ref, nw_ref, wq_ref, wk_ref, wv_ref, wg_ref, wr_ref, wgk_ref, bgk_ref,
                       gain_ref, q_ref, k_ref, g_ref, v_ref, gt_ref,
                       xn_ref, r_ref, *, q_scale, rank, row_split):
    def project(xn, rows, first_step):
        if first_step:
            r = _dot(xn, wr_ref[...])
            head = r.astype(BF16).astype(F32)
            lane = lax.broadcasted_iota(jnp.int32, r.shape, 1)
            middle = (lane >= rank) & (lane < 2 * rank)
            r_ref[rows, :] = jnp.where(middle, r - head, head).astype(BF16)
        z = _dot(r_ref[rows, :], wgk_ref[...]) + bgk_ref[...]
        log_sig = jnp.minimum(z, 0.0) - jnp.log(1.0 + jnp.exp(-jnp.abs(z)))
        g_ref[0, rows, :], g_ref[1, rows, :], g_ref[2, rows, :] = _split3_bf16(
            log_sig * (LOG2_E / GLA_GATE_NORMALIZER))
        gt_ref[rows, :] = _silu(_dot_nt(xn, wg_ref[...])) * gain_ref[...]
        q_ref[rows, :] = _dot_nt(xn, wq_ref[...]) * q_scale
        k_ref[rows, :] = _dot_nt(xn, wk_ref[...])
        v_ref[rows, :] = _dot_nt(xn, wv_ref[...]).astype(BF16)

    _project_row_tile(x_ref, nw_ref, xn_ref, project, row_split)


def _gla_inproj(h, norm_w, w_in_t, w_r3, w_gk3, b_gk, gain_row, *, key_dim, value_dim, heads, rank,
                tm=1024, steps=4):
    m, d = h.shape
    tk, tv = key_dim // steps, value_dim // steps
    kern = functools.partial(_gla_inproj_kernel, q_scale=float((key_dim // heads) ** -0.5),
                             rank=rank, row_split=1)
    declared = (2 * _nbytes((tm, d), F32) + _nbytes((tm, d), BF16)
                + 4 * _nbytes((d, tk), BF16) + 4 * _nbytes((d, tv), BF16)
                + 2 * _nbytes((d, LANES), BF16) + _nbytes((tm, LANES), BF16)
                + 2 * (3 * _nbytes((tm, tk), BF16) + 2 * _nbytes((tm, tk), F32)
                       + _nbytes((tm, tv), F32) + _nbytes((tm, tv), BF16)))
    k_first, v_first, g_first = key_dim // tk, 2 * key_dim // tv, (2 * key_dim + value_dim) // tv
    key_out = pl.BlockSpec((tm, tk), lambda i, j: (i, j))
    val_out = pl.BlockSpec((tm, tv), lambda i, j: (i, j))
    key_f32 = jax.ShapeDtypeStruct((m, key_dim), F32)
    return pl.pallas_call(
        kern,
        grid=(m // tm, steps),
        in_specs=[
            pl.BlockSpec((tm, d), lambda i, j: (i, 0)),
            pl.BlockSpec((1, d), lambda i, j: (0, 0)),
            pl.BlockSpec((tk, d), lambda i, j: (j, 0)),
            pl.BlockSpec((tk, d), lambda i, j: (k_first + j, 0)),
            pl.BlockSpec((tv, d), lambda i, j: (v_first + j, 0)),
            pl.BlockSpec((tv, d), lambda i, j: (g_first + j, 0)),
            pl.BlockSpec((d, LANES), lambda i, j: (0, 0)),
            pl.BlockSpec((LANES, tk), lambda i, j: (0, j)),
            pl.BlockSpec((1, tk), lambda i, j: (0, j)),
            pl.BlockSpec((1, tv), lambda i, j: (0, j)),
        ],
        out_specs=[key_out, key_out, pl.BlockSpec((3, tm, tk), lambda i, j: (0, i, j)), val_out, val_out],
        out_shape=[key_f32, key_f32, jax.ShapeDtypeStruct((3, m, key_dim), BF16)] + [jax.ShapeDtypeStruct((m, value_dim), BF16),
                                     jax.ShapeDtypeStruct((m, value_dim), F32)],
        scratch_shapes=[pltpu.VMEM((tm, d), BF16), pltpu.VMEM((tm, LANES), BF16)],
        compiler_params=pltpu.CompilerParams(
            dimension_semantics=("arbitrary", "arbitrary"),
            vmem_limit_bytes=_vmem_limit(declared)),
        name="gla_inproj",
    )(h, norm_w.reshape(1, d), w_in_t, w_in_t, w_in_t, w_in_t, w_r3, w_gk3, b_gk.reshape(1, key_dim), gain_row)


def _pair_levels(c):
    i = np.arange(c)[:, None]
    j = np.arange(c)[None, :]
    x = i ^ j
    lvl = np.where(x > 0, np.floor(np.log2(np.maximum(x, 1))).astype(np.int64) + 1, 0)
    return np.where(j > i, -1, lvl).astype(np.int32)


def _row_signs(c, dk):
    r = np.arange(c)[None, :, None]
    s = np.array(SMALL_LEVELS)[:, None, None]
    return np.broadcast_to(np.where(r & s, 1.0, -1.0), (len(SMALL_LEVELS), c, dk)).astype(np.float32)


def _reference_rows(b, b_ref, row, s, c):
    dk = b.shape[-1]
    if s == 1:
        return jnp.where((row & 1) != 0, pltpu.roll(b, 1, 0), b)
    if s == 2:
        lo = [jnp.broadcast_to(b_ref[pl.ds(8 * v + 1, 1), :], (8, dk)) for v in range(c // 8)]
        hi = [jnp.broadcast_to(b_ref[pl.ds(8 * v + 5, 1), :], (8, dk)) for v in range(c // 8)]
        return jnp.where((row & 4) != 0, jnp.concatenate(hi, axis=0), jnp.concatenate(lo, axis=0))
    parts = [jnp.broadcast_to(b_ref[pl.ds(m * 2 * s + s - 1, 1), :], (2 * s, dk))
             for m in range(c // (2 * s))]
    return parts[0] if len(parts) == 1 else jnp.concatenate(parts, axis=0)


def _level_exponent(b, b_ref, sgn_ref, row, s, c):
    dk = b.shape[-1]
    if s in SMALL_LEVELS:
        return (b - _reference_rows(b, b_ref, row, s, c)) * sgn_ref[SMALL_LEVELS.index(s)]
    parts = []
    for m in range(c // (2 * s)):
        lo = m * 2 * s
        r = jnp.broadcast_to(b_ref[pl.ds(lo + s - 1, 1), :], (s, dk))
        parts += [r - b[lo:lo + s], b[lo + s:lo + 2 * s] - r]
    return jnp.concatenate(parts, axis=0)


def _gla_chunk_kernel(*refs, chunk, n_chunks, group, n_cast):
    lvl_ref, sgn_ref, q_ref, k_ref, g_ref, v_ref, gt_ref = refs[:7]
    cast_src = refs[7:7 + n_cast]
    o_ref = refs[7 + n_cast]
    cast_dst = refs[8 + n_cast:8 + 2 * n_cast]
    st_ref, b0_ref, b1_ref, op_ref = refs[8 + 2 * n_cast:]
    c = chunk

    for src, dst in zip(cast_src, cast_dst):
        dst[...] = src[...].astype(dst.dtype)
    dk = q_ref.shape[-1] // group
    dv = v_ref.shape[-1] // group

    @pl.when(pl.program_id(2) == 0)
    def _():
        st_ref[...] = jnp.zeros_like(st_ref)
        op_ref[...] = jnp.zeros_like(op_ref)

    lvl = lvl_ref[...]
    tril = jnp.where(lvl >= 0, 1.0, 0.0).astype(BF16)
    row = lax.broadcasted_iota(jnp.int32, (c, dk), 0)
    heads = range(group)
    kcols = [slice(hh * dk, (hh + 1) * dk) for hh in heads]
    vcols = [slice(hh * dv, (hh + 1) * dv) for hh in heads]

    def chunk_rows(ci):
        return pl.ds(pl.multiple_of(ci * c, c), c)

    def cumulative_decay(rows, b_ref):
        for hh in heads:
            b_ref[hh] = (_dot(tril, g_ref[0, rows, kcols[hh]])
                         + (_dot(tril, g_ref[1, rows, kcols[hh]]) + _dot(tril, g_ref[2, rows, kcols[hh]])))

    def finish(rows):
        for hh in heads:
            o = op_ref[hh]
            ms = jnp.mean(o * o, axis=-1, keepdims=True)
            on = o * lax.rsqrt(ms + NORM_EPS) * gt_ref[rows, vcols[hh]]
            o_ref[rows, vcols[hh]] = on.astype(o_ref.dtype)

    def chunk_step(ci, b_cur_ref, b_next_ref):
        rows = chunk_rows(ci)
        cumulative_decay(chunk_rows(jnp.minimum(ci + 1, n_chunks - 1)), b_next_ref)
        finish(chunk_rows(jnp.maximum(ci - 1, 0)))

        for hh in heads:
            q = q_ref[rows, kcols[hh]]
            k = k_ref[rows, kcols[hh]]
            b = b_cur_ref[hh]
            b_last = b_cur_ref[hh, pl.ds(c - 1, 1), :]
            st = st_ref[hh]
            op_ref[hh] = _dot_nt((q * jnp.exp2(b)).astype(BF16), st.astype(BF16))
            kl = (k * jnp.exp2(b_last - b)).astype(BF16)
            st_ref[hh] = st * jnp.exp2(b_last) + _dot_tn(v_ref[rows, vcols[hh]], kl)

        scores = []
        for hh in heads:
            q = q_ref[rows, kcols[hh]].astype(BF16)
            k = k_ref[rows, kcols[hh]].astype(BF16)
            b = b_cur_ref[hh]
            sc = jnp.where(lvl == 0, _dot_nt(q, k), 0.0)
            s = 1
            level = 1
            while s < c:
                e = jnp.exp2(_level_exponent(b, b_cur_ref.at[hh], sgn_ref, row, s, c)).astype(BF16)
                p = _dot_nt(q * e, k * e)
                sc = jnp.where(lvl == level, p, sc)
                s *= 2
                level += 1
            scores.append(sc.astype(BF16))
        for hh in heads:
            op_ref[hh] += _dot(scores[hh], v_ref[rows, vcols[hh]])

    def body(pair, carry):
        chunk_step(2 * pair, b0_ref, b1_ref)
        chunk_step(2 * pair + 1, b1_ref, b0_ref)
        return carry

    cumulative_decay(chunk_rows(0), b0_ref)
    lax.fori_loop(0, n_chunks // 2, body, 0)
    finish(chunk_rows(n_chunks - 1))


def _gla_chunk(q, k, g_split, v, gate, *, batch, heads, group, rows_per_step, chunk=CHUNK,
               cast_jobs=()):
    m, kd = q.shape
    vd = v.shape[1]
    dk, dv = kd // heads, vd // heads
    gk, gv = group * dk, group * dv
    seq = m // batch
    tc = min(rows_per_step, seq)
    steps = seq // tc
    assert tc % (2 * chunk) == 0
    kern = functools.partial(_gla_chunk_kernel, chunk=chunk, n_chunks=tc // chunk, group=group,
                             n_cast=len(cast_jobs))
    lvl = jnp.asarray(_pair_levels(chunk))
    sgn = jnp.asarray(_row_signs(chunk, dk))

    def rows_map(b, h, t):
        return b * steps + t, h

    declared = (2 * (3 * _nbytes((tc, gk), BF16) + 2 * _nbytes((tc, gk), F32)
                     + 2 * _nbytes((tc, gv), BF16) + _nbytes((tc, gv), F32))
                + _nbytes((group, dv, dk), F32) + 2 * _nbytes((group, chunk, dk), F32)
                + _nbytes((group, chunk, dv), F32))
    key_spec = pl.BlockSpec((tc, gk), rows_map)
    val_spec = pl.BlockSpec((tc, gv), rows_map)

    grid = (batch, heads // group, steps)
    n_steps = batch * (heads // group) * steps

    def flat_step(b, h, t):
        return (b * (heads // group) + h) * steps + t

    cast_in_specs, cast_out_specs, cast_out_shapes = [], [], []
    for w, layer, rows in cast_jobs:
        cols = w.shape[2]
        rb = rows // n_steps
        assert rb * n_steps == rows and rb % 16 == 0, (w.shape, n_steps)
        cast_in_specs.append(pl.BlockSpec(
            (None, rb, cols), functools.partial(lambda b, h, t, layer: (layer, flat_step(b, h, t), 0),
                                                layer=layer)))
        cast_out_specs.append(pl.BlockSpec((rb, cols), lambda b, h, t: (flat_step(b, h, t), 0)))
        cast_out_shapes.append(jax.ShapeDtypeStruct((rows, cols), BF16))
        declared += 2 * (_nbytes((rb, cols), F32) + _nbytes((rb, cols), BF16))

    outs = pl.pallas_call(
        kern,
        grid=grid,
        in_specs=[pl.BlockSpec(lvl.shape, lambda b, h, t: (0, 0)),
                  pl.BlockSpec(sgn.shape, lambda b, h, t: (0, 0, 0)), key_spec, key_spec,
                  pl.BlockSpec((3, tc, gk), lambda b, h, t: (0, b * steps + t, h)), val_spec, val_spec]
                 + cast_in_specs,
        out_specs=[val_spec] + cast_out_specs,
        out_shape=[jax.ShapeDtypeStruct((m, vd), BF16)] + cast_out_shapes,
        scratch_shapes=[pltpu.VMEM((group, dv, dk), F32), pltpu.VMEM((group, chunk, dk), F32),
                        pltpu.VMEM((group, chunk, dk), F32), pltpu.VMEM((group, chunk, dv), F32)],
        compiler_params=pltpu.CompilerParams(
            dimension_semantics=("arbitrary", "arbitrary", "arbitrary"),
            vmem_limit_bytes=_vmem_limit(declared)),
        name="gla_chunk",
    )(lvl, sgn, q, k, g_split, v, gate, *[job[0] for job in cast_jobs])
    return outs[0], outs[1:]


def _outproj_kernel(o_ref, w_ref, h_ref, nw_ref, out_ref):
    tm = out_ref.shape[0]
    for r in range(0, tm, tm // 2):
        rows = slice(r, r + tm // 2)
        mixed = _dot(o_ref[rows, :], w_ref[...])
        out_ref[rows, :] = h_ref[rows, :] + _rms_scale(mixed, nw_ref[...])


def _outproj(o, w_out, h, norm_w, *, tm=1024):
    m, d = h.shape
    dv = o.shape[1]
    declared = (2 * _nbytes((tm, dv), BF16) + 2 * _nbytes((dv, d), BF16) + 4 * _nbytes((tm, d), F32))
    return pl.pallas_call(
        _outproj_kernel,
        grid=(m // tm,),
        in_specs=[
            pl.BlockSpec((tm, dv), lambda i: (i, 0)),
            pl.BlockSpec((dv, d), lambda i: (0, 0)),
            pl.BlockSpec((tm, d), lambda i: (i, 0)),
            pl.BlockSpec((1, d), lambda i: (0, 0)),
        ],
        out_specs=pl.BlockSpec((tm, d), lambda i: (i, 0)),
        out_shape=jax.ShapeDtypeStruct((m, d), F32),
        compiler_params=pltpu.CompilerParams(
            dimension_semantics=("arbitrary",),
            vmem_limit_bytes=_vmem_limit(declared)),
        name="outproj",
    )(o, w_out, h, norm_w.reshape(1, d))


def _mlp_kernel(h_ref, pre_ref, wu_ref, wd_ref, post_ref, out_ref, xn_ref):
    kstep = pl.program_id(1)
    last = pl.num_programs(1) - 1
    tm = out_ref.shape[0]
    halves = [slice(r, r + tm // MLP_ROW_SPLIT) for r in range(0, tm, tm // MLP_ROW_SPLIT)]

    def hidden(xn):
        u = jnp.maximum(_dot(xn, wu_ref[...]), 0.0)
        return (u * u).astype(BF16)

    def down(u2, rows, add_to):
        for n in range(0, out_ref.shape[1], MLP_DOWN_COLS):
            cols = slice(n, n + MLP_DOWN_COLS)
            part = _dot(u2, wd_ref[:, cols])
            out_ref[rows, cols] = part if add_to is None else add_to[rows, cols] + part

    @pl.when(kstep == 0)
    def _():
        for rows in halves:
            xn = _rms_scale(h_ref[rows, :], pre_ref[...]).astype(BF16)
            xn_ref[rows, :] = xn
            down(hidden(xn), rows, None)

    @pl.when((kstep > 0) & (kstep < last))
    def _():
        down(hidden(xn_ref[...]), slice(None), out_ref)

    @pl.when(kstep == last)
    def _():
        for rows in halves:
            down(hidden(xn_ref[rows, :]), rows, out_ref)
            out_ref[rows, :] = h_ref[rows, :] + _rms_scale(out_ref[rows, :], post_ref[...])


def _mlp(h, pre_w, w_up, w_down, post_w, *, tm=1024, tk=1024):
    m, d = h.shape
    dff = w_up.shape[1]
    declared = (4 * _nbytes((tm, d), F32) + 2 * _nbytes((d, tk), BF16) + 2 * _nbytes((tk, d), BF16)
                + _nbytes((tm, d), BF16) + 2 * _nbytes((tm, tk), F32))
    return pl.pallas_call(
        _mlp_kernel,
        grid=(m // tm, dff // tk),
        in_specs=[
            pl.BlockSpec((tm, d), lambda i, k: (i, 0)),
            pl.BlockSpec((1, d), lambda i, k: (0, 0)),
            pl.BlockSpec((d, tk), lambda i, k: (0, k)),
            pl.BlockSpec((tk, d), lambda i, k: (k, 0)),
            pl.BlockSpec((1, d), lambda i, k: (0, 0)),
        ],
        out_specs=pl.BlockSpec((tm, d), lambda i, k: (i, 0)),
        out_shape=jax.ShapeDtypeStruct((m, d), F32),
        scratch_shapes=[pltpu.VMEM((tm, d), BF16)],
        compiler_params=pltpu.CompilerParams(
            dimension_semantics=("arbitrary", "arbitrary"),
            vmem_limit_bytes=_vmem_limit(declared)),
        name="mlp",
    )(h, pre_w.reshape(1, d), w_up, w_down, post_w.reshape(1, d))


def kernel(x, norm_mix_pre, norm_mix_post, norm_mlp_pre, norm_mlp_post, hgrn_w_in, hgrn_lb_logits, hgrn_norm, hgrn_w_out, gla_w_in, gla_w_gk, gla_b_gk, gla_norm, gla_w_out, mlp_w_up, mlp_w_down):
    batch, seq, d = x.shape
    depth = norm_mix_pre.shape[0]
    h = x.reshape(batch * seq, d)
    later = {}
    for layer in range(depth):
        j = layer // 2
        if layer % 2 == 0:
            if layer > 0:
                later["hgrn_in", j] = hgrn_w_in
            later["hgrn_out", j] = hgrn_w_out
        else:
            later["gla_in", j] = jnp.swapaxes(gla_w_in, 1, 2)
            later["gla_out", j] = gla_w_out
        later["mlp_up", layer] = mlp_w_up
        later["mlp_down", layer] = mlp_w_down
    bf16_w = {("hgrn_in", 0): hgrn_w_in[0].astype(BF16)}
    gla_rank = gla_w_gk.shape[1]
    gla_rows = gla_w_in.shape[2] - gla_rank

    for layer in range(depth):
        j = layer // 2
        cast_jobs = [(w, key[1], gla_rows if key[0] == "gla_in" else w.shape[1])
                     for key, w in later.items()] if layer == 0 else []
        if layer % 2 == 0:
            gain_row = jnp.tile(hgrn_norm[j], HGRN_HEADS).reshape(1, -1)
            q, k, g, v, gate = _hgrn_inproj(h, norm_mix_pre[layer], bf16_w["hgrn_in", j],
                                                     hgrn_lb_logits, gain_row, layer)
            o, cast = _gla_chunk(q, k, g, v, gate, batch=batch, heads=HGRN_HEADS,
                                 group=16, rows_per_step=256, cast_jobs=cast_jobs)
            w_out = ("hgrn_out", j)
        else:
            key_dim = gla_w_gk.shape[2]
            rank = gla_w_gk.shape[1]
            value_dim = (gla_w_in.shape[2] - 2 * key_dim - rank) // 2
            n_main = 2 * key_dim + 2 * value_dim
            assert 3 * rank <= LANES
            w_r3 = jnp.pad(jnp.tile(gla_w_in[j, :, n_main:], (1, 3)),
                           ((0, 0), (0, LANES - 3 * rank))).astype(BF16)
            gk_hi = gla_w_gk[j].astype(BF16)
            gk_lo = (gla_w_gk[j] - gk_hi.astype(F32)).astype(BF16)
            w_gk3 = jnp.pad(jnp.concatenate([gk_hi, gk_hi, gk_lo], axis=0), ((0, LANES - 3 * rank), (0, 0)))
            gain_row = jnp.tile(gla_norm[j], GLA_HEADS).reshape(1, -1)
            q, k, g, v, gate = _gla_inproj(h, norm_mix_pre[layer], bf16_w["gla_in", j], w_r3,
                                                    w_gk3, gla_b_gk[j], gain_row, key_dim=key_dim,
                                                    value_dim=value_dim, heads=GLA_HEADS, rank=rank)
            o, cast = _gla_chunk(q, k, g, v, gate, batch=batch, heads=GLA_HEADS,
                                 group=4, rows_per_step=512, chunk=128, cast_jobs=cast_jobs)
            w_out = ("gla_out", j)
        if layer == 0:
            bf16_w.update(zip(later.keys(), cast))
        h = _outproj(o, bf16_w[w_out], h, norm_mix_post[layer])
        h = _mlp(h, norm_mlp_pre[layer], bf16_w["mlp_up", layer], bf16_w["mlp_down", layer],
                 norm_mlp_post[layer])
    return h.reshape(batch, seq, d)
```
